```python
import math
import jax
import jax.numpy as jnp
from jax import lax
import numpy as np

D_MODEL = 1024
BATCH = 16
SEQ = 2048
DEPTH = 2

GRID_W = 64
CTX_LEN = 256
NA_HEADS = 8
NA_HEAD_DIM = 64
NA_WIN_H = 8
NA_WIN_W = 16
DIFF_HEADS = 4
DIFF_DIM = 64
A_WIDTH = NA_HEADS * NA_HEAD_DIM
B_WIDTH = DIFF_HEADS * 2 * DIFF_DIM
ATTN_IN = 3 * (A_WIDTH + B_WIDTH)
Q_BLOCK = 128
ROPE_THETA = 10000.0
SC_WIDTH = 512
SC_K = 3
CF_WIDTH = 512
CF_K = 31
CONV_IN = 3 * SC_WIDTH + 2 * CF_WIDTH
N_EXPERTS = 16
D_EXPERT = 2816
CAPACITY_FACTOR = 2
EPS = 1e-6

kernel_name = 'hybrid_flow_backbone'


def rmsnorm(x, g):
    xf = x.astype(jnp.float32)
    xf = xf * lax.rsqrt(jnp.mean(xf * xf, axis=-1, keepdims=True) + EPS)
    return xf.astype(x.dtype) * g


def layernorm(x, g, b):
    xf = x.astype(jnp.float32)
    mu = jnp.mean(xf, axis=-1, keepdims=True)
    var = jnp.mean(jnp.square(xf - mu), axis=-1, keepdims=True)
    return ((xf - mu) * lax.rsqrt(var + EPS)).astype(x.dtype) * g + b


def modulate(h, shift, scale):
    return h * (1 + scale[..., None, :]) + shift[..., None, :]


def axial_rope(x, row, col):
    half = x.shape[-1] // 2
    n_freq = half // 2
    freqs = ROPE_THETA ** (-jnp.arange(n_freq, dtype=jnp.float32) / n_freq)
    bshape = (x.shape[1],) + (1,) * (x.ndim - 3) + (n_freq,)

    def rotate(xa, pos):
        ang = (pos.astype(jnp.float32)[:, None] * freqs).reshape(bshape)
        cos = jnp.cos(ang).astype(x.dtype)
        sin = jnp.sin(ang).astype(x.dtype)
        x1, x2 = xa[..., :n_freq], xa[..., n_freq:]
        return jnp.concatenate([x1 * cos - x2 * sin, x1 * sin + x2 * cos], axis=-1)

    return jnp.concatenate([rotate(x[..., :half], row), rotate(x[..., half:], col)], axis=-1)


def neighbourhood_attention(q, k, v, k_ctx, v_ctx, rpb):
    bn, n, h, d = q.shape
    rows = n // GRID_W
    kh = min(NA_WIN_H, rows)
    kw = NA_WIN_W
    scale = d ** -0.5
    qg = q.reshape(bn, rows, GRID_W, h, d)
    kg = k.reshape(bn, rows, GRID_W, h, d)
    vg = v.reshape(bn, rows, GRID_W, h, d)
    col = np.arange(GRID_W)
    col_idx = np.clip(col - kw // 2, 0, GRID_W - kw)[:, None] + np.arange(kw)[None, :]
    dc = col_idx - col[:, None] + (NA_WIN_W - 1)
    rpb_c = rpb[:, :, dc]

    def row_fn(r):
        rs = jnp.clip(r - kh // 2, 0, rows - kh)
        q_r = lax.dynamic_index_in_dim(qg, r, axis=1, keepdims=False)
        k_band = lax.dynamic_slice_in_dim(kg, rs, kh, axis=1)
        v_band = lax.dynamic_slice_in_dim(vg, rs, kh, axis=1)
        k_win = k_band[:, :, col_idx]
        v_win = v_band[:, :, col_idx]
        dr = rs + jnp.arange(kh) - r + (NA_WIN_H - 1)
        bias = jnp.take(rpb_c, dr, axis=1).transpose(0, 2, 1, 3)
        s_win = (jnp.einsum('bqhd,brqchd->bhqrc', q_r, k_win).astype(jnp.float32) * scale
                 + bias[None].astype(jnp.float32))
        s_ctx = jnp.einsum('bqhd,bkhd->bhqk', q_r, k_ctx).astype(jnp.float32) * scale
        s = jnp.concatenate([s_win.reshape(bn, h, GRID_W, kh * kw), s_ctx], axis=-1)
        p = jax.nn.softmax(s, axis=-1).astype(v.dtype)
        p_win = p[..., :kh * kw].reshape(bn, h, GRID_W, kh, kw)
        p_ctx = p[..., kh * kw:]
        return (jnp.einsum('bhqrc,brqchd->bqhd', p_win, v_win)
                + jnp.einsum('bhqk,bkhd->bqhd', p_ctx, v_ctx))

    out = lax.map(row_fn, jnp.arange(rows))
    return out.transpose(1, 0, 2, 3, 4).reshape(bn, n, h, d)


def differential_attention(q, k, v, k_ctx, v_ctx, lam, lam_init, subln):
    bn, n, h, _, d = q.shape
    scale = d ** -0.5
    k_all = jnp.concatenate([k, k_ctx], axis=1)
    v_all = jnp.concatenate([v, v_ctx], axis=1)
    qb = q.reshape(bn, n // Q_BLOCK, Q_BLOCK, h, 2, d).transpose(1, 0, 2, 3, 4, 5)

    def block_fn(q_blk):
        s = jnp.einsum('bqhmd,bkhmd->bhmqk', q_blk, k_all).astype(jnp.float32) * scale
        p = jax.nn.softmax(s, axis=-1)
        a = (p[:, :, 0] - lam * p[:, :, 1]).astype(v_all.dtype)
        return jnp.einsum('bhqk,bkhe->bqhe', a, v_all)

    out = lax.map(block_fn, qb).transpose(1, 0, 2, 3, 4).reshape(bn, n, h, 2 * d)
    return rmsnorm(out, subln) * (1.0 - lam_init)


def attention_mixers(h, hc, w_in, w_out, rpb, lam_q1, lam_k1, lam_q2, lam_k2, subln, lam_init):
    bn, n, _ = h.shape
    nc = hc.shape[1]
    z = h @ w_in
    cuts = np.cumsum([A_WIDTH, B_WIDTH, A_WIDTH, B_WIDTH, A_WIDTH])
    aq, bq, ak, bk, av, bv = jnp.split(z, cuts, axis=-1)
    zc = hc @ w_in[:, A_WIDTH + B_WIDTH:]
    ak_c, bk_c, av_c, bv_c = jnp.split(zc, np.cumsum([A_WIDTH, B_WIDTH, A_WIDTH]), axis=-1)
    t = jnp.arange(n)
    row, col = t // GRID_W, t % GRID_W
    y_a = neighbourhood_attention(
        aq.reshape(bn, n, NA_HEADS, NA_HEAD_DIM), ak.reshape(bn, n, NA_HEADS, NA_HEAD_DIM),
        av.reshape(bn, n, NA_HEADS, NA_HEAD_DIM), ak_c.reshape(bn, nc, NA_HEADS, NA_HEAD_DIM),
        av_c.reshape(bn, nc, NA_HEADS, NA_HEAD_DIM), rpb)
    lam = (jnp.exp(jnp.sum((lam_q1 * lam_k1).astype(jnp.float32)))
           - jnp.exp(jnp.sum((lam_q2 * lam_k2).astype(jnp.float32))) + lam_init)
    qd = axial_rope(bq.reshape(bn, n, DIFF_HEADS, 2, DIFF_DIM), row, col)
    kd = axial_rope(bk.reshape(bn, n, DIFF_HEADS, 2, DIFF_DIM), row, col)
    y_b = differential_attention(
        qd, kd, bv.reshape(bn, n, DIFF_HEADS, 2 * DIFF_DIM),
        bk_c.reshape(bn, nc, DIFF_HEADS, 2, DIFF_DIM), bv_c.reshape(bn, nc, DIFF_HEADS, 2 * DIFF_DIM),
        lam, lam_init, subln)
    y = jnp.concatenate([y_a.reshape(bn, n, A_WIDTH), y_b.reshape(bn, n, B_WIDTH)], axis=-1)
    return y @ w_out


def depthwise_conv(x, w):
    kk = w.shape[0]
    return lax.conv_general_dilated(
        x, w[:, None, :], window_strides=(1,), padding=[(kk // 2, kk // 2)],
        dimension_numbers=('NWC', 'WIO', 'NWC'), feature_group_count=x.shape[-1])


def conv_mixers(h, w_in, w_out, sc_w, cf_w, cf_b, ln_g, ln_b):
    z = h @ w_in
    cuts = np.cumsum([SC_WIDTH, SC_WIDTH, SC_WIDTH, CF_WIDTH])
    xc, gate_b, gate_c, glu_a, glu_g = jnp.split(z, cuts, axis=-1)
    y_c = gate_b * depthwise_conv(gate_c * xc, sc_w)
    u = glu_a * jax.nn.sigmoid(glu_g)
    u = depthwise_conv(u, cf_w) + cf_b
    y_d = jax.nn.silu(layernorm(u, ln_g, ln_b))
    return jnp.concatenate([y_c, y_d], axis=-1) @ w_out


def expert_choice_ffn(h, w_router, w1, w3, w2):
    bn, n, d = h.shape
    cap = CAPACITY_FACTOR * n // N_EXPERTS
    aff = jax.nn.softmax((h @ w_router).astype(jnp.float32), axis=-1)
    g, idx = lax.top_k(aff.transpose(0, 2, 1), cap)
    xin = jax.vmap(lambda hb, ib: hb[ib])(h, idx)
    hid = (jax.nn.silu(jnp.einsum('becd,edf->becf', xin, w1))
           * jnp.einsum('becd,edf->becf', xin, w3))
    y = jnp.einsum('becf,efd->becd', hid, w2) * g[..., None].astype(h.dtype)
    return jax.vmap(
        lambda ib, yb: jnp.zeros((n, d), yb.dtype).at[ib.reshape(-1)].add(yb.reshape(-1, d))
    )(idx, y)


def setup_inputs(seed: int = 0) -> dict:
    key = jax.random.key(seed)
    keys = iter(jax.random.split(key, 48))
    D = D_MODEL

    def nrm(shape, scale):
        return jax.random.normal(next(keys), shape, jnp.float32) * scale

    def gain(n):
        return 1.0 + nrm((n,), 0.02)

    inp = {}
    inp['x'] = nrm((BATCH, SEQ, D), 1.0)
    inp['c'] = nrm((BATCH, D), 1.0)
    inp['ctx'] = nrm((BATCH, CTX_LEN, D), 1.0)
    inp['c_ctx'] = nrm((D,), 1.0)
    inp['l0_norm1'] = gain(D)
    inp['l0_w_mod'] = nrm((D, 6 * D), 0.5 * D ** -0.5)
    inp['l0_b_mod'] = nrm((6 * D,), 0.01)
    inp['l0_w_in'] = nrm((D, ATTN_IN), D ** -0.5)
    inp['l0_w_out'] = nrm((A_WIDTH + B_WIDTH, D), (A_WIDTH + B_WIDTH) ** -0.5)
    inp['l0_rpb'] = nrm((NA_HEADS, 2 * NA_WIN_H - 1, 2 * NA_WIN_W - 1), 0.1)
    inp['l0_lam_q1'] = nrm((DIFF_DIM,), 0.1)
    inp['l0_lam_k1'] = nrm((DIFF_DIM,), 0.1)
    inp['l0_lam_q2'] = nrm((DIFF_DIM,), 0.1)
    inp['l0_lam_k2'] = nrm((DIFF_DIM,), 0.1)
    inp['l0_subln'] = gain(2 * DIFF_DIM)
    inp['l0_norm2'] = gain(D)
    inp['l0_w_router'] = nrm((D, N_EXPERTS), D ** -0.5)
    inp['l0_w1'] = nrm((N_EXPERTS, D, D_EXPERT), D ** -0.5)
    inp['l0_w3'] = nrm((N_EXPERTS, D, D_EXPERT), D ** -0.5)
    inp['l0_w2'] = nrm((N_EXPERTS, D_EXPERT, D), D_EXPERT ** -0.5)
    inp['l1_norm1'] = gain(D)
    inp['l1_w_mod'] = nrm((D, 6 * D), 0.5 * D ** -0.5)
    inp['l1_b_mod'] = nrm((6 * D,), 0.01)
    inp['l1_w_in'] = nrm((D, CONV_IN), D ** -0.5)
    inp['l1_w_out'] = nrm((SC_WIDTH + CF_WIDTH, D), (SC_WIDTH + CF_WIDTH) ** -0.5)
    inp['l1_sc_w'] = nrm((SC_K, SC_WIDTH), SC_K ** -0.5)
    inp['l1_cf_w'] = nrm((CF_K, CF_WIDTH), CF_K ** -0.5)
    inp['l1_cf_b'] = nrm((CF_WIDTH,), 0.01)
    inp['l1_ln_g'] = gain(CF_WIDTH)
    inp['l1_ln_b'] = nrm((CF_WIDTH,), 0.01)
    inp['l1_norm2'] = gain(D)
    inp['l1_w_router'] = nrm((D, N_EXPERTS), D ** -0.5)
    inp['l1_w1'] = nrm((N_EXPERTS, D, D_EXPERT), D ** -0.5)
    inp['l1_w3'] = nrm((N_EXPERTS, D, D_EXPERT), D ** -0.5)
    inp['l1_w2'] = nrm((N_EXPERTS, D_EXPERT, D), D_EXPERT ** -0.5)
    inp['final_norm'] = gain(D)
    return inp


def reference(x, c, ctx, c_ctx,
              l0_norm1, l0_w_mod, l0_b_mod, l0_w_in, l0_w_out, l0_rpb,
              l0_lam_q1, l0_lam_k1, l0_lam_q2, l0_lam_k2, l0_subln,
              l0_norm2, l0_w_router, l0_w1, l0_w3, l0_w2,
              l1_norm1, l1_w_mod, l1_b_mod, l1_w_in, l1_w_out,
              l1_sc_w, l1_cf_w, l1_cf_b, l1_ln_g, l1_ln_b,
              l1_norm2, l1_w_router, l1_w1, l1_w3, l1_w2,
              final_norm):
    layers = [
        dict(norm1=l0_norm1, w_mod=l0_w_mod, b_mod=l0_b_mod, w_in=l0_w_in, w_out=l0_w_out,
             rpb=l0_rpb, lam_q1=l0_lam_q1, lam_k1=l0_lam_k1, lam_q2=l0_lam_q2, lam_k2=l0_lam_k2,
             subln=l0_subln, norm2=l0_norm2, w_router=l0_w_router, w1=l0_w1, w3=l0_w3, w2=l0_w2),
        dict(norm1=l1_norm1, w_mod=l1_w_mod, b_mod=l1_b_mod, w_in=l1_w_in, w_out=l1_w_out,
             sc_w=l1_sc_w, cf_w=l1_cf_w, cf_b=l1_cf_b, ln_g=l1_ln_g, ln_b=l1_ln_b,
             norm2=l1_norm2, w_router=l1_w_router, w1=l1_w1, w3=l1_w3, w2=l1_w2),
    ]
    silu_c = jax.nn.silu(c)
    silu_c_ctx = jax.nn.silu(c_ctx)
    for l in range(DEPTH):
        p = layers[l]
        mod = silu_c @ p['w_mod'] + p['b_mod']
        shift1, scale1, gate1, shift2, scale2, gate2 = jnp.split(mod, 6, axis=-1)
        h = modulate(rmsnorm(x, p['norm1']), shift1, scale1)
        if l % 2 == 0:
            ctx_mod = silu_c_ctx @ p['w_mod'][:, :2 * D_MODEL] + p['b_mod'][:2 * D_MODEL]
            ctx_shift, ctx_scale = jnp.split(ctx_mod, 2)
            hc = modulate(rmsnorm(ctx, p['norm1']), ctx_shift, ctx_scale)
            lam_init = 0.8 - 0.6 * math.exp(-0.3 * l)
            y = attention_mixers(h, hc, p['w_in'], p['w_out'], p['rpb'], p['lam_q1'], p['lam_k1'],
                                 p['lam_q2'], p['lam_k2'], p['subln'], lam_init)
        else:
            y = conv_mixers(h, p['w_in'], p['w_out'], p['sc_w'], p['cf_w'], p['cf_b'],
                            p['ln_g'], p['ln_b'])
        x = x + gate1[:, None, :] * y
        h = modulate(rmsnorm(x, p['norm2']), shift2, scale2)
        x = x + gate2[:, None, :] * expert_choice_ffn(h, p['w_router'], p['w1'], p['w3'], p['w2'])
    return rmsnorm(x, final_norm)
```

```python
import functools
import math

import numpy as np
import jax
import jax.numpy as jnp
from jax import lax
from jax.experimental import pallas as pl
from jax.experimental.pallas import tpu as pltpu

D_MODEL = 1024
BATCH = 16
SEQ = 2048
GRID_W = 64
GRID_H = SEQ // GRID_W
CTX_LEN = 256
NA_HEADS = 8
NA_WIN_H = 8
NA_WIN_W = 16
DIFF_HEADS = 4
DIFF_DIM = 64
HEAD_GROUP = 512
ATTN_IN = 6 * HEAD_GROUP
ROPE_THETA = 10000.0
SC_K = 3
CF_K = 31
CONV_W = 512
CONV_IN = 5 * CONV_W
N_EXPERTS = 16
D_EXPERT = 2816
CAP = 2 * SEQ // N_EXPERTS
EPS = 1e-6
LAM_INIT0 = 0.8 - 0.6 * math.exp(-0.3 * 0)
NEG = -1e30

LANES = 128
NA_ROWS = 4
NA_BAND = 12
PROJ_TN = 512
DIFF_TQ = 256
CONV_TN = 256
CONV_HALO = 16
FFN_TF = 256
FFN_M_SPLIT = 2
COMBINE_TN = 512
VMEM_LIMIT = 60 * 1024 * 1024

F32 = jnp.float32
BF16 = jnp.bfloat16


def _dot(a, b):
    return jnp.dot(a, b, preferred_element_type=F32)


def _dot_nt(a, b):
    return lax.dot_general(a, b, (((1,), (1,)), ((), ())), preferred_element_type=F32)


def _dot_tn(a, b):
    return lax.dot_general(a, b, (((0,), (0,)), ((), ())), preferred_element_type=F32)


def _sigmoid(x):
    return 1.0 / (1.0 + jnp.exp(-x))


def _params(*sem):
    return pltpu.CompilerParams(dimension_semantics=sem, vmem_limit_bytes=VMEM_LIMIT)


def _norm_mod(x, g, shift, scale):
    ms = jnp.mean(x * x, axis=-1, keepdims=True)
    return (x * lax.rsqrt(ms + EPS)) * g * (1.0 + scale) + shift


def _mod_kernel(c_ref, w_ref, b_ref, o_ref):
    cc = c_ref[...]
    s = cc * _sigmoid(cc)
    o_ref[...] = _dot(s.astype(BF16), w_ref[...].astype(BF16)) + b_ref[...]


def _modulation(cc, w_mod, b_mod):
    rows = cc.shape[0]
    tc = D_MODEL
    return pl.pallas_call(
        _mod_kernel,
        grid=(w_mod.shape[1] // tc,),
        in_specs=[pl.BlockSpec((rows, D_MODEL), lambda j: (0, 0)),
                  pl.BlockSpec((D_MODEL, tc), lambda j: (0, j)),
                  pl.BlockSpec((1, tc), lambda j: (0, j))],
        out_specs=pl.BlockSpec((rows, tc), lambda j: (0, j)),
        out_shape=jax.ShapeDtypeStruct((rows, w_mod.shape[1]), F32),
        compiler_params=_params("arbitrary"),
        name="modulation",
    )(cc, w_mod, b_mod.reshape(1, -1))


def _rope(z, cos, sin):
    lane = lax.broadcasted_iota(jnp.int32, z.shape, 1)
    partner = jnp.where((lane & 31) < 16, pltpu.roll(z, LANES - 16, 1), pltpu.roll(z, 16, 1))
    return z * cos + partner * sin


def _attn_proj_kernel(x_ref, g_ref, sh_ref, sc_ref, w_ref, cos_ref, sin_ref, o_ref, *, rope_groups, scaled_groups):
    h = _norm_mod(x_ref[0], g_ref[...], sh_ref[0], sc_ref[0]).astype(BF16)
    n_groups = w_ref.shape[1] // HEAD_GROUP
    for j in range(n_groups):
        z = _dot(h, w_ref[:, j * HEAD_GROUP:(j + 1) * HEAD_GROUP])
        for s in range(HEAD_GROUP // LANES):
            zz = z[:, s * LANES:(s + 1) * LANES]
            if j in rope_groups:
                zz = _rope(zz, cos_ref[...], sin_ref[...])
            if j in scaled_groups:
                zz = zz * (DIFF_DIM ** -0.5)
            o_ref[0, :, j * HEAD_GROUP + s * LANES:j * HEAD_GROUP + (s + 1) * LANES] = zz.astype(BF16)


def _attn_proj(x, g, shift, scale, w, cos_t, sin_t, rope_groups, scaled_groups):
    bn, n, _ = x.shape
    tn = min(PROJ_TN, n)
    cols = w.shape[1]
    return pl.pallas_call(
        functools.partial(_attn_proj_kernel, rope_groups=rope_groups, scaled_groups=scaled_groups),
        grid=(bn, n // tn),
        in_specs=[pl.BlockSpec((1, tn, D_MODEL), lambda b, i: (b, i, 0)),
                  pl.BlockSpec((1, D_MODEL), lambda b, i: (0, 0)),
                  pl.BlockSpec((1, 1, D_MODEL), lambda b, i: (b, 0, 0)),
                  pl.BlockSpec((1, 1, D_MODEL), lambda b, i: (b, 0, 0)),
                  pl.BlockSpec((D_MODEL, cols), lambda b, i: (0, 0)),
                  pl.BlockSpec((tn, LANES), lambda b, i: (i, 0)),
                  pl.BlockSpec((tn, LANES), lambda b, i: (i, 0))],
        out_specs=pl.BlockSpec((1, tn, cols), lambda b, i: (b, i, 0)),
        out_shape=jax.ShapeDtypeStruct((bn, n, cols), BF16),
        compiler_params=_params("parallel", "parallel"),
        name="attn_proj",
    )(x, g.reshape(1, -1), shift, scale, w, cos_t, sin_t)


def _rope_tables():
    n_freq = DIFF_DIM // 4
    freqs = ROPE_THETA ** (-jnp.arange(n_freq, dtype=F32) / n_freq)
    t = jnp.arange(SEQ)
    row, col = t // GRID_W, t % GRID_W

    def half(pos):
        ang = pos.astype(F32)[:, None] * freqs
        c, s = jnp.cos(ang), jnp.sin(ang)
        return jnp.concatenate([c, c], axis=-1), jnp.concatenate([-s, s], axis=-1)

    cr, sr = half(row)
    cc, sc = half(col)
    cos64 = jnp.concatenate([cr, cc], axis=-1)
    sin64 = jnp.concatenate([sr, sc], axis=-1)
    return jnp.tile(cos64, (1, LANES // DIFF_DIM)), jnp.tile(sin64, (1, LANES // DIFF_DIM))


def _na_bias_table(rpb):
    c = np.arange(GRID_W)
    cs = np.clip(c - NA_WIN_W // 2, 0, GRID_W - NA_WIN_W)
    kc = np.arange(GRID_W)
    col_ok = (kc[None, :] >= cs[:, None]) & (kc[None, :] < cs[:, None] + NA_WIN_W)
    dc = kc[None, :] - c[:, None] + NA_WIN_W - 1
    one_hot = (dc[None] == np.arange(2 * NA_WIN_W - 1)[:, None, None]) & col_ok[None]
    toeplitz = jnp.einsum('hrd,dck->hrck', rpb.astype(F32), jnp.asarray(one_hot, F32),
                          precision=lax.Precision.HIGHEST)
    toeplitz = jnp.where(jnp.asarray(col_ok)[None, None], toeplitz, NEG)
    blocks = np.array([0, 2 * NA_ROWS, GRID_H - NA_ROWS])
    starts = np.clip(blocks - NA_WIN_H // 2, 0, GRID_H - NA_BAND)
    r = blocks[:, None] + np.arange(NA_ROWS)[None, :]
    rs = np.clip(r - NA_WIN_H // 2, 0, GRID_H - NA_WIN_H)
    krow = starts[:, None, None] + np.arange(NA_BAND)[None, None, :]
    row_ok = (krow >= rs[:, :, None]) & (krow < rs[:, :, None] + NA_WIN_H)
    dr = np.clip(krow - r[:, :, None] + NA_WIN_H - 1, 0, 2 * NA_WIN_H - 2)
    picked = jnp.take(toeplitz, jnp.asarray(dr.reshape(-1)), axis=1)
    picked = picked.reshape(NA_HEADS, len(blocks), NA_ROWS, NA_BAND, GRID_W, GRID_W)
    picked = jnp.where(jnp.asarray(row_ok)[None, :, :, :, None, None], picked, NEG)
    return picked.transpose(1, 0, 2, 4, 3, 5).reshape(len(blocks), NA_HEADS, NA_ROWS * GRID_W, NA_BAND * GRID_W)


def _na_band_start(i):
    return jnp.clip(i * NA_ROWS - NA_WIN_H // 2, 0, GRID_H - NA_BAND)


def _na_kernel(q_ref, k_ref, v_ref, kc_ref, vc_ref, bias_ref, o_ref):
    start = pl.multiple_of(_na_band_start(pl.program_id(1)) * GRID_W, GRID_W)
    band = NA_BAND * GRID_W
    lane = lax.broadcasted_iota(jnp.int32, (q_ref.shape[1], LANES), 1)
    first = lane < (LANES // 2)
    for p in range(NA_HEADS // 2):
        cols = slice(p * LANES, (p + 1) * LANES)
        q2 = q_ref[0, :, cols]
        kb = k_ref[0, pl.ds(start, band), cols]
        vb = v_ref[0, pl.ds(start, band), cols]
        kc = kc_ref[0, :, cols]
        vc = vc_ref[0, :, cols]
        outs = []
        for par in range(2):
            qm = jnp.where(first if par == 0 else jnp.logical_not(first), q2, jnp.zeros_like(q2))
            s_w = _dot_nt(qm, kb) + bias_ref[0, 2 * p + par]
            s_c = _dot_nt(qm, kc)
            m = jnp.maximum(jnp.max(s_w, axis=-1, keepdims=True), jnp.max(s_c, axis=-1, keepdims=True))
            p_w = jnp.exp(s_w - m)
            p_c = jnp.exp(s_c - m)
            l = jnp.sum(p_w, axis=-1, keepdims=True) + jnp.sum(p_c, axis=-1, keepdims=True)
            o = _dot(p_w.astype(BF16), vb) + _dot(p_c.astype(BF16), vc)
            outs.append(o / l)
        o_ref[0, :, cols] = jnp.where(first, outs[0], outs[1]).astype(BF16)


def _na_block_class(i):
    return (i * NA_ROWS - _na_band_start(i)) // NA_ROWS


def _na_attention(z, zc, bias):
    bn = z.shape[0]
    tq = NA_ROWS * GRID_W
    return pl.pallas_call(
        _na_kernel,
        grid=(bn, GRID_H // NA_ROWS),
        in_specs=[pl.BlockSpec((1, tq, HEAD_GROUP), lambda b, i: (b, i, 0)),
                  pl.BlockSpec((1, SEQ, HEAD_GROUP), lambda b, i: (b, 0, 2)),
                  pl.BlockSpec((1, SEQ, HEAD_GROUP), lambda b, i: (b, 0, 4)),
                  pl.BlockSpec((1, CTX_LEN, HEAD_GROUP), lambda b, i: (b, 0, 0)),
                  pl.BlockSpec((1, CTX_LEN, HEAD_GROUP), lambda b, i: (b, 0, 2)),
                  pl.BlockSpec((1, NA_HEADS, tq, NA_BAND * GRID_W), lambda b, i: (_na_block_class(i), 0, 0, 0))],
        out_specs=pl.BlockSpec((1, tq, HEAD_GROUP), lambda b, i: (b, i, 0)),
        out_shape=jax.ShapeDtypeStruct((bn, SEQ, HEAD_GROUP), BF16),
        compiler_params=_params("parallel", "arbitrary"),
        name="na_attention",
    )(z, z, z, zc, zc, bias)


def _diff_kernel(lq1_ref, lk1_ref, lq2_ref, lk2_ref, sub_ref, q_ref, k_ref, v_ref, kc_ref, vc_ref, o_ref):
    lam = (jnp.exp(jnp.sum(lq1_ref[...] * lk1_ref[...], axis=-1, keepdims=True))
           - jnp.exp(jnp.sum(lq2_ref[...] * lk2_ref[...], axis=-1, keepdims=True)) + LAM_INIT0)
    tq = q_ref.shape[1]
    lane = lax.broadcasted_iota(jnp.int32, (tq, LANES), 1)
    first = lane < DIFF_DIM
    for h in range(DIFF_HEADS):
        cols = slice(h * LANES, (h + 1) * LANES)
        q2 = q_ref[0, :, cols]
        k2 = k_ref[0, :, cols]
        kc2 = kc_ref[0, :, cols]
        maps = []
        for mp in range(2):
            qm = jnp.where(first if mp == 0 else jnp.logical_not(first), q2, jnp.zeros_like(q2))
            s_l = _dot_nt(qm, k2)
            s_c = _dot_nt(qm, kc2)
            m = jnp.maximum(jnp.max(s_l, axis=-1, keepdims=True), jnp.max(s_c, axis=-1, keepdims=True))
            p_l = jnp.exp(s_l - m)
            p_c = jnp.exp(s_c - m)
            inv = 1.0 / (jnp.sum(p_l, axis=-1, keepdims=True) + jnp.sum(p_c, axis=-1, keepdims=True))
            maps.append((p_l * inv, p_c * inv))
        a_l = (maps[0][0] - lam * maps[1][0]).astype(BF16)
        a_c = (maps[0][1] - lam * maps[1][1]).astype(BF16)
        o = _dot(a_l, v_ref[0, :, cols]) + _dot(a_c, vc_ref[0, :, cols])
        ms = jnp.mean(o * o, axis=-1, keepdims=True)
        o = (o * lax.rsqrt(ms + EPS)) * sub_ref[...] * (1.0 - LAM_INIT0)
        o_ref[0, :, cols] = o.astype(BF16)


def _diff_attention(z, zc, lq1, lk1, lq2, lk2, subln):
    bn = z.shape[0]
    tq = DIFF_TQ
    vec = pl.BlockSpec((1, DIFF_DIM), lambda b, i: (0, 0))
    return pl.pallas_call(
        _diff_kernel,
        grid=(bn, SEQ // tq),
        in_specs=[vec, vec, vec, vec,
                  pl.BlockSpec((1, 2 * DIFF_DIM), lambda b, i: (0, 0)),
                  pl.BlockSpec((1, tq, HEAD_GROUP), lambda b, i: (b, i, 1)),
                  pl.BlockSpec((1, SEQ, HEAD_GROUP), lambda b, i: (b, 0, 3)),
                  pl.BlockSpec((1, SEQ, HEAD_GROUP), lambda b, i: (b, 0, 5)),
                  pl.BlockSpec((1, CTX_LEN, HEAD_GROUP), lambda b, i: (b, 0, 1)),
                  pl.BlockSpec((1, CTX_LEN, HEAD_GROUP), lambda b, i: (b, 0, 3))],
        out_specs=pl.BlockSpec((1, tq, HEAD_GROUP), lambda b, i: (b, i, 0)),
        out_shape=jax.ShapeDtypeStruct((bn, SEQ, HEAD_GROUP), BF16),
        compiler_params=_params("parallel", "arbitrary"),
        name="diff_attention",
    )(lq1.reshape(1, -1), lk1.reshape(1, -1), lq2.reshape(1, -1), lk2.reshape(1, -1), subln.reshape(1, -1),
      z, z, z, zc, zc)


def _out_kernel(x_ref, ya_ref, yb_ref, w_ref, g1_ref, n2_ref, sh2_ref, sc2_ref, wr_ref, x1_ref, h2_ref, aff_ref):
    half = ya_ref.shape[2]
    y = _dot(ya_ref[0], w_ref[:half, :]) + _dot(yb_ref[0], w_ref[half:, :])
    x1 = x_ref[0] + g1_ref[0] * y
    x1_ref[0] = x1
    hb = _norm_mod(x1, n2_ref[...], sh2_ref[0], sc2_ref[0]).astype(BF16)
    h2_ref[0] = hb
    logits = _dot_nt(wr_ref[...], hb)
    m = jnp.max(logits, axis=0, keepdims=True)
    p = jnp.exp(logits - m)
    aff_ref[0] = p / jnp.sum(p, axis=0, keepdims=True)


def _out_proj(x, ya, yb, w_out, gate1, norm2, shift2, scale2, w_router_t):
    bn, n, _ = x.shape
    tn = PROJ_TN
    half = ya.shape[2]
    tok = lambda b, i: (b, i, 0)
    per_b = lambda b, i: (b, 0, 0)
    return pl.pallas_call(
        _out_kernel,
        grid=(bn, n // tn),
        in_specs=[pl.BlockSpec((1, tn, D_MODEL), tok),
                  pl.BlockSpec((1, tn, half), tok),
                  pl.BlockSpec((1, tn, half), tok),
                  pl.BlockSpec((2 * half, D_MODEL), lambda b, i: (0, 0)),
                  pl.BlockSpec((1, 1, D_MODEL), per_b),
                  pl.BlockSpec((1, D_MODEL), lambda b, i: (0, 0)),
                  pl.BlockSpec((1, 1, D_MODEL), per_b),
                  pl.BlockSpec((1, 1, D_MODEL), per_b),
                  pl.BlockSpec((N_EXPERTS, D_MODEL), lambda b, i: (0, 0))],
        out_specs=[pl.BlockSpec((1, tn, D_MODEL), tok),
                   pl.BlockSpec((1, tn, D_MODEL), tok),
                   pl.BlockSpec((1, N_EXPERTS, tn), lambda b, i: (b, 0, i))],
        out_shape=[jax.ShapeDtypeStruct((bn, n, D_MODEL), F32),
                   jax.ShapeDtypeStruct((bn, n, D_MODEL), BF16),
                   jax.ShapeDtypeStruct((bn, N_EXPERTS, n), F32)],
        compiler_params=_params("parallel", "parallel"),
        name="out_proj",
    )(x, ya, yb, w_out, gate1, norm2.reshape(1, -1), shift2, scale2, w_router_t)


def _excl_cumsum_lanes(x01):
    e, n = x01.shape
    i = lax.broadcasted_iota(jnp.int32, (LANES, LANES), 0)
    j = lax.broadcasted_iota(jnp.int32, (LANES, LANES), 1)
    tri = jnp.where(i < j, 1.0, 0.0).astype(BF16)
    offset = jnp.zeros((e, 1), F32)
    chunks = []
    for c in range(n // LANES):
        xc = x01[:, c * LANES:(c + 1) * LANES]
        chunks.append(_dot(xc.astype(BF16), tri) + offset)
        offset = offset + jnp.sum(xc, axis=-1, keepdims=True)
    return jnp.concatenate(chunks, axis=-1), offset


def _route_kernel(aff_ref, pos_ref, g_ref, post_ref):
    a = aff_ref[0]
    thr = jnp.zeros((a.shape[0], 1), jnp.int32)
    for bit in range(30, -1, -1):
        cand = thr | (1 << bit)
        cnt = jnp.sum(jnp.where(a >= lax.bitcast_convert_type(cand, F32), 1.0, 0.0), axis=-1, keepdims=True)
        thr = jnp.where(cnt >= CAP, cand, thr)
    gt = jnp.where(a >= lax.bitcast_convert_type(thr + 1, F32), 1.0, 0.0)
    eq = jnp.where(a >= lax.bitcast_convert_type(thr, F32), 1.0, 0.0) - gt
    need = CAP - jnp.sum(gt, axis=-1, keepdims=True)
    eq_rank, _ = _excl_cumsum_lanes(eq)
    sel = gt + eq * jnp.where(eq_rank < need, 1.0, 0.0)
    pos, _ = _excl_cumsum_lanes(sel)
    pos = jnp.where(sel > 0.5, pos, -1.0)
    pos_ref[0] = pos.astype(jnp.int32)
    g_ref[0] = jnp.where(sel > 0.5, a, 0.0)
    padded = jnp.concatenate([pos, jnp.full((LANES - pos.shape[0], pos.shape[1]), -1.0, F32)], axis=0)
    post_ref[0] = padded.T.astype(jnp.int32)


def _route(aff_t):
    bn, e, n = aff_t.shape
    blk = pl.BlockSpec((1, e, n), lambda b: (b, 0, 0))
    return pl.pallas_call(
        _route_kernel,
        grid=(bn,),
        in_specs=[blk],
        out_specs=[blk, blk, pl.BlockSpec((1, n, LANES), lambda b: (b, 0, 0))],
        out_shape=[jax.ShapeDtypeStruct((bn, e, n), jnp.int32), jax.ShapeDtypeStruct((bn, e, n), F32),
                   jax.ShapeDtypeStruct((bn, n, LANES), jnp.int32)],
        compiler_params=_params("parallel"),
        name="route",
    )(aff_t)


def _one_hot_slots(pos_row):
    slot = lax.broadcasted_iota(jnp.int32, (CAP, pos_row.shape[1]), 0)
    return slot == pos_row


def _gather_kernel(pos_ref, g_ref, h_ref, x_ref, gs_ref):
    for e in range(N_EXPERTS):
        hit = _one_hot_slots(pos_ref[0, e:e + 1, :])
        x_ref[e] = _dot(jnp.where(hit, 1.0, 0.0).astype(BF16), h_ref[0]).astype(BF16)
        g = jnp.sum(jnp.where(hit, g_ref[0, e:e + 1, :], 0.0), axis=-1, keepdims=True)
        gs_ref[e] = jnp.broadcast_to(g, (CAP, LANES))


def _gather(pos, gsel, h2):
    bn, e, n = pos.shape
    rows = pl.BlockSpec((1, e, n), lambda b: (b, 0, 0))
    return pl.pallas_call(
        _gather_kernel,
        grid=(bn,),
        in_specs=[rows, rows, pl.BlockSpec((1, n, D_MODEL), lambda b: (b, 0, 0))],
        out_specs=[pl.BlockSpec((e, CAP, D_MODEL), lambda b: (0, b, 0)),
                   pl.BlockSpec((e, CAP, LANES), lambda b: (0, b, 0))],
        out_shape=[jax.ShapeDtypeStruct((e, bn * CAP, D_MODEL), BF16),
                   jax.ShapeDtypeStruct((e, bn * CAP, LANES), F32)],
        compiler_params=_params("parallel"),
        name="moe_gather",
    )(pos, gsel, h2)


def _ffn_kernel(x_ref, gs_ref, w1_ref, w3_ref, w2_ref, y_ref, acc_ref):
    f = pl.program_id(2)
    n_f = pl.num_programs(2)
    xs = x_ref[0]
    a = _dot(xs, w1_ref[0].astype(BF16))
    hid = (a * _sigmoid(a)) * _dot(xs, w3_ref[0].astype(BF16))
    y = _dot(hid.astype(BF16), w2_ref[0].astype(BF16))

    @pl.when(f == 0)
    def _first():
        acc_ref[...] = y

    @pl.when(f > 0)
    def _rest():
        acc_ref[...] += y

    @pl.when(f == n_f - 1)
    def _emit():
        g = gs_ref[0]
        for j in range(D_MODEL // LANES):
            cols = slice(j * LANES, (j + 1) * LANES)
            y_ref[0, :, cols] = (acc_ref[:, cols] * g).astype(BF16)


def _expert_ffn(xg, gs, w1, w3, w2):
    e, rows, _ = xg.shape
    tm = rows // FFN_M_SPLIT
    n_f = D_EXPERT // FFN_TF
    return pl.pallas_call(
        _ffn_kernel,
        grid=(e, FFN_M_SPLIT, n_f),
        in_specs=[pl.BlockSpec((1, tm, D_MODEL), lambda ex, m, f: (ex, m, 0)),
                  pl.BlockSpec((1, tm, LANES), lambda ex, m, f: (ex, m, 0)),
                  pl.BlockSpec((1, D_MODEL, FFN_TF), lambda ex, m, f: (ex, 0, f)),
                  pl.BlockSpec((1, D_MODEL, FFN_TF), lambda ex, m, f: (ex, 0, f)),
                  pl.BlockSpec((1, FFN_TF, D_MODEL), lambda ex, m, f: (ex, f, 0))],
        out_specs=pl.BlockSpec((1, tm, D_MODEL), lambda ex, m, f: (ex, m, 0)),
        out_shape=jax.ShapeDtypeStruct((e, rows, D_MODEL), BF16),
        scratch_shapes=[pltpu.VMEM((tm, D_MODEL), F32)],
        compiler_params=_params("parallel", "parallel", "arbitrary"),
        name="expert_ffn",
    )(xg, gs, w1, w3, w2)


def _combine_kernel(post_ref, y_ref, x_ref, g2_ref, fin_ref, o_ref, *, final):
    tn = post_ref.shape[1]
    slot = lax.broadcasted_iota(jnp.int32, (tn, CAP), 1)
    hits = [jnp.where(post_ref[0, :, e:e + 1] == slot, 1.0, 0.0).astype(BF16) for e in range(N_EXPERTS)]
    moe = _dot(jnp.concatenate(hits, axis=-1), y_ref[...].reshape(N_EXPERTS * CAP, D_MODEL))
    x = x_ref[0] + g2_ref[0] * moe
    if final:
        ms = jnp.mean(x * x, axis=-1, keepdims=True)
        x = (x * lax.rsqrt(ms + EPS)) * fin_ref[...]
    o_ref[0] = x


def _combine(post, y, x, gate2, final_g, final):
    bn, n, _ = x.shape
    tn = COMBINE_TN
    tok = lambda b, i: (b, i, 0)
    return pl.pallas_call(
        functools.partial(_combine_kernel, final=final),
        grid=(bn, n // tn),
        in_specs=[pl.BlockSpec((1, tn, LANES), tok),
                  pl.BlockSpec((N_EXPERTS, CAP, D_MODEL), lambda b, i: (0, b, 0)),
                  pl.BlockSpec((1, tn, D_MODEL), tok),
                  pl.BlockSpec((1, 1, D_MODEL), lambda b, i: (b, 0, 0)),
                  pl.BlockSpec((1, D_MODEL), lambda b, i: (0, 0))],
        out_specs=pl.BlockSpec((1, tn, D_MODEL), tok),
        out_shape=jax.ShapeDtypeStruct((bn, n, D_MODEL), F32),
        compiler_params=_params("parallel", "arbitrary"),
        name="moe_combine",
    )(post, y, x, gate2, final_g.reshape(1, -1))


def _conv_proj_kernel(x_ref, g_ref, sh_ref, sc_ref, w_ref, v_ref, gb_ref, u_ref):
    h = _norm_mod(x_ref[0], g_ref[...], sh_ref[0], sc_ref[0]).astype(BF16)
    w = CONV_W
    xc = _dot(h, w_ref[:, 0 * w:1 * w])
    gate_c = _dot(h, w_ref[:, 2 * w:3 * w])
    v_ref[0] = gate_c * xc
    gb_ref[0] = _dot(h, w_ref[:, 1 * w:2 * w])
    glu_a = _dot(h, w_ref[:, 3 * w:4 * w])
    glu_g = _dot(h, w_ref[:, 4 * w:5 * w])
    u_ref[0] = glu_a * _sigmoid(glu_g)


def _conv_proj(x, g, shift, scale, w):
    bn, n, _ = x.shape
    tn = PROJ_TN
    tok = lambda b, i: (b, i, 0)
    per_b = lambda b, i: (b, 0, 0)
    narrow = pl.BlockSpec((1, tn, CONV_W), tok)
    vec = pl.BlockSpec((1, 1, D_MODEL), per_b)
    return pl.pallas_call(
        _conv_proj_kernel,
        grid=(bn, n // tn),
        in_specs=[pl.BlockSpec((1, tn, D_MODEL), tok),
                  pl.BlockSpec((1, D_MODEL), lambda b, i: (0, 0)),
                  vec, vec,
                  pl.BlockSpec((D_MODEL, CONV_IN), lambda b, i: (0, 0))],
        out_specs=[narrow, narrow, narrow],
        out_shape=[jax.ShapeDtypeStruct((bn, n, CONV_W), F32)] * 3,
        compiler_params=_params("parallel", "parallel"),
        name="conv_proj",
    )(x, g.reshape(1, -1), shift, scale, w)


def _conv_kernel(v_ref, gb_ref, u_ref, scw_ref, cfw_ref, cfb_ref, lng_ref, lnb_ref, yc_ref, yd_ref, vpad_ref, upad_ref):
    i = pl.program_id(1)
    n = v_ref.shape[1]
    tn = gb_ref.shape[1]

    @pl.when(i == 0)
    def _stage():
        zeros = jnp.zeros((CONV_HALO, CONV_W), F32)
        for pad_ref, src_ref in ((vpad_ref, v_ref), (upad_ref, u_ref)):
            pad_ref[0:CONV_HALO, :] = zeros
            pad_ref[CONV_HALO:CONV_HALO + n, :] = src_ref[0]
            pad_ref[CONV_HALO + n:, :] = zeros

    base = pl.multiple_of(i * tn, tn)
    yd_parts = []
    for s in range(CONV_W // LANES):
        cols = slice(s * LANES, (s + 1) * LANES)
        vwin = vpad_ref[pl.ds(base, tn + 2 * CONV_HALO), cols]
        acc = None
        for k in range(SC_K):
            off = CONV_HALO + k - SC_K // 2
            term = vwin[off:off + tn, :] * scw_ref[k:k + 1, cols]
            acc = term if acc is None else acc + term
        yc_ref[0, :, cols] = (gb_ref[0, :, cols] * acc).astype(BF16)
        uwin = upad_ref[pl.ds(base, tn + 2 * CONV_HALO), cols]
        acc = None
        for k in range(CF_K):
            off = CONV_HALO + k - CF_K // 2
            term = uwin[off:off + tn, :] * cfw_ref[k:k + 1, cols]
            acc = term if acc is None else acc + term
        yd_parts.append(acc + cfb_ref[:, cols])
    u = jnp.concatenate(yd_parts, axis=-1)
    mu = jnp.mean(u, axis=-1, keepdims=True)
    d = u - mu
    var = jnp.mean(d * d, axis=-1, keepdims=True)
    ln = (d * lax.rsqrt(var + EPS)) * lng_ref[...] + lnb_ref[...]
    yd_ref[0] = (ln * _sigmoid(ln)).astype(BF16)


def _convs(v, gb, u, sc_w, cf_w, cf_b, ln_g, ln_b):
    bn, n, w = v.shape
    tn = CONV_TN
    full = pl.BlockSpec((1, n, w), lambda b, i: (b, 0, 0))
    tile = pl.BlockSpec((1, tn, w), lambda b, i: (b, i, 0))
    vec = pl.BlockSpec((1, w), lambda b, i: (0, 0))
    return pl.pallas_call(
        _conv_kernel,
        grid=(bn, n // tn),
        in_specs=[full, tile, full,
                  pl.BlockSpec((SC_K, w), lambda b, i: (0, 0)),
                  pl.BlockSpec((CF_K, w), lambda b, i: (0, 0)),
                  vec, vec, vec],
        out_specs=[tile, tile],
        out_shape=[jax.ShapeDtypeStruct((bn, n, w), BF16)] * 2,
        scratch_shapes=[pltpu.VMEM((n + 2 * CONV_HALO, w), F32)] * 2,
        compiler_params=_params("parallel", "arbitrary"),
        name="convs",
    )(v, gb, u, sc_w, cf_w, cf_b.reshape(1, -1), ln_g.reshape(1, -1), ln_b.reshape(1, -1))


def _split_mod(mod, bn):
    return [mod[:bn, j * D_MODEL:(j + 1) * D_MODEL].reshape(bn, 1, D_MODEL) for j in range(6)]


def _moe(x, gate2, aff_t, h2, w1, w3, w2, final_g, final):
    pos, gsel, post = _route(aff_t)
    xg, gs = _gather(pos, gsel, h2)
    y = _expert_ffn(xg, gs, w1, w3, w2)
    return _combine(post, y, x, gate2, final_g, final)


def kernel(x, c, ctx, c_ctx, l0_norm1, l0_w_mod, l0_b_mod, l0_w_in, l0_w_out, l0_rpb, l0_lam_q1, l0_lam_k1, l0_lam_q2, l0_lam_k2, l0_subln, l0_norm2, l0_w_router, l0_w1, l0_w3, l0_w2, l1_norm1, l1_w_mod, l1_b_mod, l1_w_in, l1_w_out, l1_sc_w, l1_cf_w, l1_cf_b, l1_ln_g, l1_ln_b, l1_norm2, l1_w_router, l1_w1, l1_w3, l1_w2, final_norm):
    bn = x.shape[0]
    pad_rows = -(bn + 1) % 8
    cc = jnp.concatenate([c, c_ctx[None, :], jnp.zeros((pad_rows, D_MODEL), F32)], axis=0)

    mod0 = _modulation(cc, l0_w_mod, l0_b_mod)
    shift1, scale1, gate1, shift2, scale2, gate2 = _split_mod(mod0, bn)
    ctx_shift = jnp.broadcast_to(mod0[bn, :D_MODEL].reshape(1, 1, D_MODEL), (bn, 1, D_MODEL))
    ctx_scale = jnp.broadcast_to(mod0[bn, D_MODEL:2 * D_MODEL].reshape(1, 1, D_MODEL), (bn, 1, D_MODEL))
    w_in0 = l0_w_in.astype(BF16)
    cos_t, sin_t = _rope_tables()
    z = _attn_proj(x, l0_norm1, shift1, scale1, w_in0, cos_t, sin_t, rope_groups=(1, 3), scaled_groups=(0, 1))
    zc = _attn_proj(ctx, l0_norm1, ctx_shift, ctx_scale, w_in0[:, 2 * HEAD_GROUP:], cos_t[:CTX_LEN], sin_t[:CTX_LEN],
                    rope_groups=(), scaled_groups=())
    y_a = _na_attention(z, zc, _na_bias_table(l0_rpb))
    y_b = _diff_attention(z, zc, l0_lam_q1, l0_lam_k1, l0_lam_q2, l0_lam_k2, l0_subln)
    x1, h2, aff_t = _out_proj(x, y_a, y_b, l0_w_out.astype(BF16), gate1, l0_norm2, shift2, scale2,
                              l0_w_router.T.astype(BF16))
    x2 = _moe(x1, gate2, aff_t, h2, l0_w1, l0_w3, l0_w2, final_norm, final=False)

    mod1 = _modulation(cc, l1_w_mod, l1_b_mod)
    shift1b, scale1b, gate1b, shift2b, scale2b, gate2b = _split_mod(mod1, bn)
    v, gb, u = _conv_proj(x2, l1_norm1, shift1b, scale1b, l1_w_in.astype(BF16))
    y_c, y_d = _convs(v, gb, u, l1_sc_w, l1_cf_w, l1_cf_b, l1_ln_g, l1_ln_b)
    x3, h2b, aff_tb = _out_proj(x2, y_c, y_d, l1_w_out.astype(BF16), gate1b, l1_norm2, shift2b, scale2b,
                                l1_w_router.T.astype(BF16))
    return _moe(x3, gate2b, aff_tb, h2b, l1_w1, l1_w3, l1_w2, final_norm, final=True)
```

```python
import functools
import math

import numpy as np
import jax
import jax.numpy as jnp
from jax import lax
from jax.experimental import pallas as pl
from jax.experimental.pallas import tpu as pltpu

D_MODEL = 1024
BATCH = 16
SEQ = 2048
GRID_W = 64
GRID_H = SEQ // GRID_W
CTX_LEN = 256
NA_HEADS = 8
NA_WIN_H = 8
NA_WIN_W = 16
DIFF_HEADS = 4
DIFF_DIM = 64
HEAD_GROUP = 512
ATTN_IN = 6 * HEAD_GROUP
ROPE_THETA = 10000.0
SC_K = 3
CF_K = 31
CONV_W = 512
CONV_IN = 5 * CONV_W
N_EXPERTS = 16
D_EXPERT = 2816
CAP = 2 * SEQ // N_EXPERTS
EPS = 1e-6
LAM_INIT0 = 0.8 - 0.6 * math.exp(-0.3 * 0)
NEG = -1e30
LOG2E = math.log2(math.e)
Q_SCALE = DIFF_DIM ** -0.5 * LOG2E

LANES = 128
NA_ROWS = 4
NA_BAND = 12
PROJ_TN = 512
DIFF_TQ = 256
CONV_TN = 256
CONV_HALO = 16
CONV_SUB = 128
FFN_TF = 256
FFN_M_SPLIT = 2
COMBINE_TN = 512
VMEM_LIMIT = 60 * 1024 * 1024

F32 = jnp.float32
BF16 = jnp.bfloat16


def _dot(a, b):
    return jnp.dot(a, b, preferred_element_type=F32)


def _dot_nt(a, b):
    return lax.dot_general(a, b, (((1,), (1,)), ((), ())), preferred_element_type=F32)


def _dot_tn(a, b):
    return lax.dot_general(a, b, (((0,), (0,)), ((), ())), preferred_element_type=F32)


def _sigmoid(x):
    return 1.0 / (1.0 + jnp.exp(-x))


def _params(*sem):
    return pltpu.CompilerParams(dimension_semantics=sem, vmem_limit_bytes=VMEM_LIMIT)


def _norm_mod(x, g, shift, scale):
    ms = jnp.mean(x * x, axis=-1, keepdims=True)
    return (x * lax.rsqrt(ms + EPS)) * g * (1.0 + scale) + shift


def _mod_kernel(c_ref, w_ref, b_ref, o_ref):
    cc = c_ref[...]
    s = cc * _sigmoid(cc)
    o_ref[...] = _dot(s.astype(BF16), w_ref[...].astype(BF16)) + b_ref[...]


def _modulation(cc, w_mod, b_mod):
    rows = cc.shape[0]
    tc = D_MODEL
    return pl.pallas_call(
        _mod_kernel,
        grid=(w_mod.shape[1] // tc,),
        in_specs=[pl.BlockSpec((rows, D_MODEL), lambda j: (0, 0)),
                  pl.BlockSpec((D_MODEL, tc), lambda j: (0, j)),
                  pl.BlockSpec((1, tc), lambda j: (0, j))],
        out_specs=pl.BlockSpec((rows, tc), lambda j: (0, j)),
        out_shape=jax.ShapeDtypeStruct((rows, w_mod.shape[1]), F32),
        compiler_params=_params("arbitrary"),
        name="modulation",
    )(cc, w_mod, b_mod.reshape(1, -1))


def _rope(z, cos, sin):
    lane = lax.broadcasted_iota(jnp.int32, z.shape, 1)
    partner = jnp.where((lane & 31) < 16, pltpu.roll(z, LANES - 16, 1), pltpu.roll(z, 16, 1))
    return z * cos + partner * sin


def _attn_proj_kernel(x_ref, g_ref, sh_ref, sc_ref, w_ref, cos_ref, sin_ref, o_ref, *, rope_groups, scaled_groups):
    h = _norm_mod(x_ref[0], g_ref[...], sh_ref[0], sc_ref[0]).astype(BF16)
    n_groups = w_ref.shape[1] // HEAD_GROUP
    for j in range(n_groups):
        z = _dot(h, w_ref[:, j * HEAD_GROUP:(j + 1) * HEAD_GROUP])
        for s in range(HEAD_GROUP // LANES):
            zz = z[:, s * LANES:(s + 1) * LANES]
            if j in rope_groups:
                zz = _rope(zz, cos_ref[...], sin_ref[...])
            if j in scaled_groups:
                zz = zz * Q_SCALE
            o_ref[0, :, j * HEAD_GROUP + s * LANES:j * HEAD_GROUP + (s + 1) * LANES] = zz.astype(BF16)


def _attn_proj(x, g, shift, scale, w, cos_t, sin_t, rope_groups, scaled_groups):
    bn, n, _ = x.shape
    tn = min(PROJ_TN, n)
    cols = w.shape[1]
    return pl.pallas_call(
        functools.partial(_attn_proj_kernel, rope_groups=rope_groups, scaled_groups=scaled_groups),
        grid=(bn, n // tn),
        in_specs=[pl.BlockSpec((1, tn, D_MODEL), lambda b, i: (b, i, 0)),
                  pl.BlockSpec((1, D_MODEL), lambda b, i: (0, 0)),
                  pl.BlockSpec((1, 1, D_MODEL), lambda b, i: (b, 0, 0)),
                  pl.BlockSpec((1, 1, D_MODEL), lambda b, i: (b, 0, 0)),
                  pl.BlockSpec((D_MODEL, cols), lambda b, i: (0, 0)),
                  pl.BlockSpec((tn, LANES), lambda b, i: (i, 0)),
                  pl.BlockSpec((tn, LANES), lambda b, i: (i, 0))],
        out_specs=pl.BlockSpec((1, tn, cols), lambda b, i: (b, i, 0)),
        out_shape=jax.ShapeDtypeStruct((bn, n, cols), BF16),
        compiler_params=_params("parallel", "parallel"),
        name="attn_proj",
    )(x, g.reshape(1, -1), shift, scale, w, cos_t, sin_t)


def _rope_tables():
    n_freq = DIFF_DIM // 4
    freqs = ROPE_THETA ** (-jnp.arange(n_freq, dtype=F32) / n_freq)
    t = jnp.arange(SEQ)
    row, col = t // GRID_W, t % GRID_W

    def half(pos):
        ang = pos.astype(F32)[:, None] * freqs
        c, s = jnp.cos(ang), jnp.sin(ang)
        return jnp.concatenate([c, c], axis=-1), jnp.concatenate([-s, s], axis=-1)

    cr, sr = half(row)
    cc, sc = half(col)
    cos64 = jnp.concatenate([cr, cc], axis=-1)
    sin64 = jnp.concatenate([sr, sc], axis=-1)
    return jnp.tile(cos64, (1, LANES // DIFF_DIM)), jnp.tile(sin64, (1, LANES // DIFF_DIM))


def _na_bias_table(rpb):
    c = np.arange(GRID_W)
    cs = np.clip(c - NA_WIN_W // 2, 0, GRID_W - NA_WIN_W)
    kc = np.arange(GRID_W)
    col_ok = (kc[None, :] >= cs[:, None]) & (kc[None, :] < cs[:, None] + NA_WIN_W)
    dc = kc[None, :] - c[:, None] + NA_WIN_W - 1
    one_hot = (dc[None] == np.arange(2 * NA_WIN_W - 1)[:, None, None]) & col_ok[None]
    toeplitz = jnp.einsum('hrd,dck->hrck', rpb.astype(F32), jnp.asarray(one_hot, F32),
                          precision=lax.Precision.HIGHEST)
    toeplitz = jnp.where(jnp.asarray(col_ok)[None, None], toeplitz * LOG2E, NEG)
    blocks = np.array([0, 2 * NA_ROWS, GRID_H - NA_ROWS])
    starts = np.clip(blocks - NA_WIN_H // 2, 0, GRID_H - NA_BAND)
    r = blocks[:, None] + np.arange(NA_ROWS)[None, :]
    rs = np.clip(r - NA_WIN_H // 2, 0, GRID_H - NA_WIN_H)
    krow = starts[:, None, None] + np.arange(NA_BAND)[None, None, :]
    row_ok = (krow >= rs[:, :, None]) & (krow < rs[:, :, None] + NA_WIN_H)
    dr = np.clip(krow - r[:, :, None] + NA_WIN_H - 1, 0, 2 * NA_WIN_H - 2)
    picked = jnp.take(toeplitz, jnp.asarray(dr.reshape(-1)), axis=1)
    picked = picked.reshape(NA_HEADS, len(blocks), NA_ROWS, NA_BAND, GRID_W, GRID_W)
    picked = jnp.where(jnp.asarray(row_ok)[None, :, :, :, None, None], picked, NEG)
    return picked.transpose(1, 0, 2, 4, 3, 5).reshape(len(blocks), NA_HEADS, NA_ROWS * GRID_W, NA_BAND * GRID_W)


def _na_band_start(i):
    return jnp.clip(i * NA_ROWS - NA_WIN_H // 2, 0, GRID_H - NA_BAND)


def _na_kernel(q_ref, k_ref, v_ref, kc_ref, vc_ref, bias_ref, o_ref):
    start = pl.multiple_of(_na_band_start(pl.program_id(1)) * GRID_W, GRID_W)
    band = NA_BAND * GRID_W
    lane = lax.broadcasted_iota(jnp.int32, (q_ref.shape[1], LANES), 1)
    first = lane < (LANES // 2)
    for p in range(NA_HEADS // 2):
        cols = slice(p * LANES, (p + 1) * LANES)
        q2 = q_ref[0, :, cols]
        kb = k_ref[0, pl.ds(start, band), cols]
        vb = v_ref[0, pl.ds(start, band), cols]
        kc = kc_ref[0, :, cols]
        vc = vc_ref[0, :, cols]
        outs = []
        for par in range(2):
            qm = jnp.where(first if par == 0 else jnp.logical_not(first), q2, jnp.zeros_like(q2))
            s_w = _dot_nt(qm, kb) + bias_ref[0, 2 * p + par]
            s_c = _dot_nt(qm, kc)
            m = jnp.maximum(jnp.max(s_w, axis=-1, keepdims=True), jnp.max(s_c, axis=-1, keepdims=True))
            p_w = jnp.exp2(s_w - m)
            p_c = jnp.exp2(s_c - m)
            l = jnp.sum(p_w, axis=-1, keepdims=True) + jnp.sum(p_c, axis=-1, keepdims=True)
            o = _dot(p_w.astype(BF16), vb) + _dot(p_c.astype(BF16), vc)
            outs.append(o / l)
        o_ref[0, :, cols] = jnp.where(first, outs[0], outs[1]).astype(BF16)


def _na_block_class(i):
    return (i * NA_ROWS - _na_band_start(i)) // NA_ROWS


def _na_attention(z, zc, bias):
    bn = z.shape[0]
    tq = NA_ROWS * GRID_W
    return pl.pallas_call(
        _na_kernel,
        grid=(bn, GRID_H // NA_ROWS),
        in_specs=[pl.BlockSpec((1, tq, HEAD_GROUP), lambda b, i: (b, i, 0)),
                  pl.BlockSpec((1, SEQ, HEAD_GROUP), lambda b, i: (b, 0, 2)),
                  pl.BlockSpec((1, SEQ, HEAD_GROUP), lambda b, i: (b, 0, 4)),
                  pl.BlockSpec((1, CTX_LEN, HEAD_GROUP), lambda b, i: (b, 0, 0)),
                  pl.BlockSpec((1, CTX_LEN, HEAD_GROUP), lambda b, i: (b, 0, 2)),
                  pl.BlockSpec((1, NA_HEADS, tq, NA_BAND * GRID_W), lambda b, i: (_na_block_class(i), 0, 0, 0))],
        out_specs=pl.BlockSpec((1, tq, HEAD_GROUP), lambda b, i: (b, i, 0)),
        out_shape=jax.ShapeDtypeStruct((bn, SEQ, HEAD_GROUP), BF16),
        compiler_params=_params("parallel", "arbitrary"),
        name="na_attention",
    )(z, z, z, zc, zc, bias)


def _diff_kernel(lq1_ref, lk1_ref, lq2_ref, lk2_ref, sub_ref, q_ref, k_ref, v_ref, kc_ref, vc_ref, o_ref):
    lam = (jnp.exp(jnp.sum(lq1_ref[...] * lk1_ref[...], axis=-1, keepdims=True))
           - jnp.exp(jnp.sum(lq2_ref[...] * lk2_ref[...], axis=-1, keepdims=True)) + LAM_INIT0)
    tq = q_ref.shape[1]
    lane = lax.broadcasted_iota(jnp.int32, (tq, LANES), 1)
    first = lane < DIFF_DIM

    def scores(h):
        cols = slice(h * LANES, (h + 1) * LANES)
        q2 = q_ref[0, :, cols]
        out = []
        for mp in range(2):
            qm = jnp.where(first if mp == 0 else jnp.logical_not(first), q2, jnp.zeros_like(q2))
            out.append((_dot_nt(qm, k_ref[0, :, cols]), _dot_nt(qm, kc_ref[0, :, cols])))
        return out

    nxt = scores(0)
    for h in range(DIFF_HEADS):
        cols = slice(h * LANES, (h + 1) * LANES)
        cur = nxt
        if h + 1 < DIFF_HEADS:
            nxt = scores(h + 1)
        maps = []
        for s_l, s_c in cur:
            m = jnp.maximum(jnp.max(s_l, axis=-1, keepdims=True), jnp.max(s_c, axis=-1, keepdims=True))
            p_l = jnp.exp2(s_l - m)
            p_c = jnp.exp2(s_c - m)
            maps.append((p_l, p_c, jnp.sum(p_l, axis=-1, keepdims=True) + jnp.sum(p_c, axis=-1, keepdims=True)))
        ratio = lam * maps[0][2] / maps[1][2]
        a_l = (maps[0][0] - ratio * maps[1][0]).astype(BF16)
        a_c = (maps[0][1] - ratio * maps[1][1]).astype(BF16)
        o = (_dot(a_l, v_ref[0, :, cols]) + _dot(a_c, vc_ref[0, :, cols])) / maps[0][2]
        ms = jnp.mean(o * o, axis=-1, keepdims=True)
        o = (o * lax.rsqrt(ms + EPS)) * sub_ref[...] * (1.0 - LAM_INIT0)
        o_ref[0, :, cols] = o.astype(BF16)


def _diff_attention(z, zc, lq1, lk1, lq2, lk2, subln):
    bn = z.shape[0]
    tq = DIFF_TQ
    vec = pl.BlockSpec((1, DIFF_DIM), lambda b, i: (0, 0))
    return pl.pallas_call(
        _diff_kernel,
        grid=(bn, SEQ // tq),
        in_specs=[vec, vec, vec, vec,
                  pl.BlockSpec((1, 2 * DIFF_DIM), lambda b, i: (0, 0)),
                  pl.BlockSpec((1, tq, HEAD_GROUP), lambda b, i: (b, i, 1)),
                  pl.BlockSpec((1, SEQ, HEAD_GROUP), lambda b, i: (b, 0, 3)),
                  pl.BlockSpec((1, SEQ, HEAD_GROUP), lambda b, i: (b, 0, 5)),
                  pl.BlockSpec((1, CTX_LEN, HEAD_GROUP), lambda b, i: (b, 0, 1)),
                  pl.BlockSpec((1, CTX_LEN, HEAD_GROUP), lambda b, i: (b, 0, 3))],
        out_specs=pl.BlockSpec((1, tq, HEAD_GROUP), lambda b, i: (b, i, 0)),
        out_shape=jax.ShapeDtypeStruct((bn, SEQ, HEAD_GROUP), BF16),
        compiler_params=_params("parallel", "arbitrary"),
        name="diff_attention",
    )(lq1.reshape(1, -1), lk1.reshape(1, -1), lq2.reshape(1, -1), lk2.reshape(1, -1), subln.reshape(1, -1),
      z, z, z, zc, zc)


def _out_kernel(x_ref, ya_ref, yb_ref, w_ref, g1_ref, n2_ref, sh2_ref, sc2_ref, wr_ref, x1_ref, h2_ref, aff_ref):
    half = ya_ref.shape[2]
    y = _dot(ya_ref[0], w_ref[:half, :]) + _dot(yb_ref[0], w_ref[half:, :])
    x1 = x_ref[0] + g1_ref[0] * y
    x1_ref[0] = x1
    hb = _norm_mod(x1, n2_ref[...], sh2_ref[0], sc2_ref[0]).astype(BF16)
    h2_ref[0] = hb
    logits = _dot_nt(wr_ref[...], hb)
    m = jnp.max(logits, axis=0, keepdims=True)
    p = jnp.exp(logits - m)
    aff_ref[0] = p / jnp.sum(p, axis=0, keepdims=True)


def _out_proj(x, ya, yb, w_out, gate1, norm2, shift2, scale2, w_router_t):
    bn, n, _ = x.shape
    tn = PROJ_TN
    half = ya.shape[2]
    tok = lambda b, i: (b, i, 0)
    per_b = lambda b, i: (b, 0, 0)
    return pl.pallas_call(
        _out_kernel,
        grid=(bn, n // tn),
        in_specs=[pl.BlockSpec((1, tn, D_MODEL), tok),
                  pl.BlockSpec((1, tn, half), tok),
                  pl.BlockSpec((1, tn, half), tok),
                  pl.BlockSpec((2 * half, D_MODEL), lambda b, i: (0, 0)),
                  pl.BlockSpec((1, 1, D_MODEL), per_b),
                  pl.BlockSpec((1, D_MODEL), lambda b, i: (0, 0)),
                  pl.BlockSpec((1, 1, D_MODEL), per_b),
                  pl.BlockSpec((1, 1, D_MODEL), per_b),
                  pl.BlockSpec((N_EXPERTS, D_MODEL), lambda b, i: (0, 0))],
        out_specs=[pl.BlockSpec((1, tn, D_MODEL), tok),
                   pl.BlockSpec((1, tn, D_MODEL), tok),
                   pl.BlockSpec((1, N_EXPERTS, tn), lambda b, i: (b, 0, i))],
        out_shape=[jax.ShapeDtypeStruct((bn, n, D_MODEL), F32),
                   jax.ShapeDtypeStruct((bn, n, D_MODEL), BF16),
                   jax.ShapeDtypeStruct((bn, N_EXPERTS, n), F32)],
        compiler_params=_params("parallel", "parallel"),
        name="out_proj",
    )(x, ya, yb, w_out, gate1, norm2.reshape(1, -1), shift2, scale2, w_router_t)


def _excl_cumsum_lanes(x01):
    e, n = x01.shape
    i = lax.broadcasted_iota(jnp.int32, (LANES, LANES), 0)
    j = lax.broadcasted_iota(jnp.int32, (LANES, LANES), 1)
    tri = jnp.where(i < j, 1.0, 0.0).astype(BF16)
    offset = jnp.zeros((e, 1), F32)
    chunks = []
    for c in range(n // LANES):
        xc = x01[:, c * LANES:(c + 1) * LANES]
        chunks.append(_dot(xc.astype(BF16), tri) + offset)
        offset = offset + jnp.sum(xc, axis=-1, keepdims=True)
    return jnp.concatenate(chunks, axis=-1), offset


def _route_kernel(aff_ref, pos_ref, g_ref, post_ref):
    a = aff_ref[0]
    thr = jnp.zeros((a.shape[0], 1), jnp.int32)
    for bit in range(30, -1, -1):
        cand = thr | (1 << bit)
        cnt = jnp.sum(jnp.where(a >= lax.bitcast_convert_type(cand, F32), 1.0, 0.0), axis=-1, keepdims=True)
        thr = jnp.where(cnt >= CAP, cand, thr)
    gt = jnp.where(a >= lax.bitcast_convert_type(thr + 1, F32), 1.0, 0.0)
    eq = jnp.where(a >= lax.bitcast_convert_type(thr, F32), 1.0, 0.0) - gt
    need = CAP - jnp.sum(gt, axis=-1, keepdims=True)
    eq_rank, _ = _excl_cumsum_lanes(eq)
    sel = gt + eq * jnp.where(eq_rank < need, 1.0, 0.0)
    pos, _ = _excl_cumsum_lanes(sel)
    pos = jnp.where(sel > 0.5, pos, -1.0)
    pos_ref[0] = pos.astype(jnp.int32)
    g_ref[0] = jnp.where(sel > 0.5, a, 0.0)
    padded = jnp.concatenate([pos, jnp.full((LANES - pos.shape[0], pos.shape[1]), -1.0, F32)], axis=0)
    post_ref[0] = padded.T.astype(jnp.int32)


def _route(aff_t):
    bn, e, n = aff_t.shape
    blk = pl.BlockSpec((1, e, n), lambda b: (b, 0, 0))
    return pl.pallas_call(
        _route_kernel,
        grid=(bn,),
        in_specs=[blk],
        out_specs=[blk, blk, pl.BlockSpec((1, n, LANES), lambda b: (b, 0, 0))],
        out_shape=[jax.ShapeDtypeStruct((bn, e, n), jnp.int32), jax.ShapeDtypeStruct((bn, e, n), F32),
                   jax.ShapeDtypeStruct((bn, n, LANES), jnp.int32)],
        compiler_params=_params("parallel"),
        name="route",
    )(aff_t)


def _one_hot_slots(pos_row):
    slot = lax.broadcasted_iota(jnp.int32, (CAP, pos_row.shape[1]), 0)
    return slot == pos_row


def _gather_kernel(pos_ref, g_ref, h_ref, x_ref, gs_ref):
    for e in range(N_EXPERTS):
        hit = _one_hot_slots(pos_ref[0, e:e + 1, :])
        x_ref[e] = _dot(jnp.where(hit, 1.0, 0.0).astype(BF16), h_ref[0]).astype(BF16)
        g = jnp.sum(jnp.where(hit, g_ref[0, e:e + 1, :], 0.0), axis=-1, keepdims=True)
        gs_ref[e] = jnp.broadcast_to(g, (CAP, LANES))


def _gather(pos, gsel, h2):
    bn, e, n = pos.shape
    rows = pl.BlockSpec((1, e, n), lambda b: (b, 0, 0))
    return pl.pallas_call(
        _gather_kernel,
        grid=(bn,),
        in_specs=[rows, rows, pl.BlockSpec((1, n, D_MODEL), lambda b: (b, 0, 0))],
        out_specs=[pl.BlockSpec((e, CAP, D_MODEL), lambda b: (0, b, 0)),
                   pl.BlockSpec((e, CAP, LANES), lambda b: (0, b, 0))],
        out_shape=[jax.ShapeDtypeStruct((e, bn * CAP, D_MODEL), BF16),
                   jax.ShapeDtypeStruct((e, bn * CAP, LANES), F32)],
        compiler_params=_params("parallel"),
        name="moe_gather",
    )(pos, gsel, h2)


def _ffn_kernel(x_ref, gs_ref, w1_ref, w3_ref, w2_ref, y_ref, acc_ref):
    f = pl.program_id(2)
    n_f = pl.num_programs(2)
    xs = x_ref[0]
    a = _dot(xs, w1_ref[0].astype(BF16))
    hid = (a * _sigmoid(a)) * _dot(xs, w3_ref[0].astype(BF16))

    @pl.when(f == 0)
    def _first():
        acc_ref[...] = jnp.zeros_like(acc_ref)

    acc_ref[...] += _dot(hid.astype(BF16), w2_ref[0].astype(BF16))

    @pl.when(f == n_f - 1)
    def _emit():
        g = gs_ref[0]
        for j in range(D_MODEL // LANES):
            cols = slice(j * LANES, (j + 1) * LANES)
            y_ref[0, :, cols] = (acc_ref[:, cols] * g).astype(BF16)


def _expert_ffn(xg, gs, w1, w3, w2):
    e, rows, _ = xg.shape
    tm = rows // FFN_M_SPLIT
    n_f = D_EXPERT // FFN_TF
    return pl.pallas_call(
        _ffn_kernel,
        grid=(e, FFN_M_SPLIT, n_f),
        in_specs=[pl.BlockSpec((1, tm, D_MODEL), lambda ex, m, f: (ex, m, 0)),
                  pl.BlockSpec((1, tm, LANES), lambda ex, m, f: (ex, m, 0)),
                  pl.BlockSpec((1, D_MODEL, FFN_TF), lambda ex, m, f: (ex, 0, f)),
                  pl.BlockSpec((1, D_MODEL, FFN_TF), lambda ex, m, f: (ex, 0, f)),
                  pl.BlockSpec((1, FFN_TF, D_MODEL), lambda ex, m, f: (ex, f, 0))],
        out_specs=pl.BlockSpec((1, tm, D_MODEL), lambda ex, m, f: (ex, m, 0)),
        out_shape=jax.ShapeDtypeStruct((e, rows, D_MODEL), BF16),
        scratch_shapes=[pltpu.VMEM((tm, D_MODEL), F32)],
        compiler_params=_params("parallel", "parallel", "arbitrary"),
        name="expert_ffn",
    )(xg, gs, w1, w3, w2)


def _combine_kernel(post_ref, y_ref, x_ref, g2_ref, fin_ref, o_ref, *, final):
    tn = post_ref.shape[1]
    slot = lax.broadcasted_iota(jnp.int32, (tn, CAP), 1)
    hits = [jnp.where(post_ref[0, :, e:e + 1] == slot, 1.0, 0.0).astype(BF16) for e in range(N_EXPERTS)]
    moe = _dot(jnp.concatenate(hits, axis=-1), y_ref[...].reshape(N_EXPERTS * CAP, D_MODEL))
    x = x_ref[0] + g2_ref[0] * moe
    if final:
        ms = jnp.mean(x * x, axis=-1, keepdims=True)
        x = (x * lax.rsqrt(ms + EPS)) * fin_ref[...]
    o_ref[0] = x


def _combine(post, y, x, gate2, final_g, final):
    bn, n, _ = x.shape
    tn = COMBINE_TN
    tok = lambda b, i: (b, i, 0)
    return pl.pallas_call(
        functools.partial(_combine_kernel, final=final),
        grid=(bn, n // tn),
        in_specs=[pl.BlockSpec((1, tn, LANES), tok),
                  pl.BlockSpec((N_EXPERTS, CAP, D_MODEL), lambda b, i: (0, b, 0)),
                  pl.BlockSpec((1, tn, D_MODEL), tok),
                  pl.BlockSpec((1, 1, D_MODEL), lambda b, i: (b, 0, 0)),
                  pl.BlockSpec((1, D_MODEL), lambda b, i: (0, 0))],
        out_specs=pl.BlockSpec((1, tn, D_MODEL), tok),
        out_shape=jax.ShapeDtypeStruct((bn, n, D_MODEL), F32),
        compiler_params=_params("parallel", "arbitrary"),
        name="moe_combine",
    )(post, y, x, gate2, final_g.reshape(1, -1))


def _conv_proj_kernel(x_ref, g_ref, sh_ref, sc_ref, w_ref, v_ref, gb_ref, u_ref):
    h = _norm_mod(x_ref[0], g_ref[...], sh_ref[0], sc_ref[0]).astype(BF16)
    w = CONV_W
    xc = _dot(h, w_ref[:, 0 * w:1 * w])
    gate_c = _dot(h, w_ref[:, 2 * w:3 * w])
    v_ref[0] = gate_c * xc
    gb_ref[0] = _dot(h, w_ref[:, 1 * w:2 * w])
    glu_a = _dot(h, w_ref[:, 3 * w:4 * w])
    glu_g = _dot(h, w_ref[:, 4 * w:5 * w])
    u_ref[0] = glu_a * _sigmoid(glu_g)


def _conv_proj(x, g, shift, scale, w):
    bn, n, _ = x.shape
    tn = PROJ_TN
    tok = lambda b, i: (b, i, 0)
    per_b = lambda b, i: (b, 0, 0)
    narrow = pl.BlockSpec((1, tn, CONV_W), tok)
    vec = pl.BlockSpec((1, 1, D_MODEL), per_b)
    return pl.pallas_call(
        _conv_proj_kernel,
        grid=(bn, n // tn),
        in_specs=[pl.BlockSpec((1, tn, D_MODEL), tok),
                  pl.BlockSpec((1, D_MODEL), lambda b, i: (0, 0)),
                  vec, vec,
                  pl.BlockSpec((D_MODEL, CONV_IN), lambda b, i: (0, 0))],
        out_specs=[narrow, narrow, narrow],
        out_shape=[jax.ShapeDtypeStruct((bn, n, CONV_W), F32)] * 3,
        compiler_params=_params("parallel", "parallel"),
        name="conv_proj",
    )(x, g.reshape(1, -1), shift, scale, w)


def _depthwise(win, w_ref, cols, taps, rows):
    offs = [CONV_HALO + k - taps // 2 for k in range(taps)]
    acc = None
    for b in range(8):
        ks = [k for k in range(taps) if offs[k] % 8 == b]
        if not ks:
            continue
        shifted = pltpu.roll(win, win.shape[0] - b, 0) if b else win
        for k in ks:
            term = shifted[offs[k] - b:offs[k] - b + rows, :] * w_ref[k:k + 1, cols]
            acc = term if acc is None else acc + term
    return acc


def _conv_kernel(v_ref, gb_ref, u_ref, scw_ref, cfw_ref, cfb_ref, lng_ref, lnb_ref, yc_ref, yd_ref, vpad_ref, upad_ref):
    i = pl.program_id(1)
    n = v_ref.shape[1]
    tn = gb_ref.shape[1]

    @pl.when(i == 0)
    def _stage():
        zeros = jnp.zeros((CONV_HALO, CONV_W), F32)
        for pad_ref, src_ref in ((vpad_ref, v_ref), (upad_ref, u_ref)):
            pad_ref[0:CONV_HALO, :] = zeros
            pad_ref[CONV_HALO:CONV_HALO + n, :] = src_ref[0]
            pad_ref[CONV_HALO + n:, :] = zeros

    for j in range(tn // CONV_SUB):
        rows = slice(j * CONV_SUB, (j + 1) * CONV_SUB)
        start = pl.multiple_of(i * tn + j * CONV_SUB, CONV_SUB)
        yd_parts = []
        for s in range(CONV_W // LANES):
            cols = slice(s * LANES, (s + 1) * LANES)
            vwin = vpad_ref[pl.ds(start, CONV_SUB + 2 * CONV_HALO), cols]
            yc = gb_ref[0, rows, cols] * _depthwise(vwin, scw_ref, cols, SC_K, CONV_SUB)
            yc_ref[0, rows, cols] = yc.astype(BF16)
            uwin = upad_ref[pl.ds(start, CONV_SUB + 2 * CONV_HALO), cols]
            yd_parts.append(_depthwise(uwin, cfw_ref, cols, CF_K, CONV_SUB) + cfb_ref[:, cols])
        u = jnp.concatenate(yd_parts, axis=-1)
        mu = jnp.mean(u, axis=-1, keepdims=True)
        d = u - mu
        var = jnp.mean(d * d, axis=-1, keepdims=True)
        ln = (d * lax.rsqrt(var + EPS)) * lng_ref[...] + lnb_ref[...]
        yd_ref[0, rows, :] = (ln * _sigmoid(ln)).astype(BF16)


def _convs(v, gb, u, sc_w, cf_w, cf_b, ln_g, ln_b):
    bn, n, w = v.shape
    tn = CONV_TN
    full = pl.BlockSpec((1, n, w), lambda b, i: (b, 0, 0))
    tile = pl.BlockSpec((1, tn, w), lambda b, i: (b, i, 0))
    vec = pl.BlockSpec((1, w), lambda b, i: (0, 0))
    return pl.pallas_call(
        _conv_kernel,
        grid=(bn, n // tn),
        in_specs=[full, tile, full,
                  pl.BlockSpec((SC_K, w), lambda b, i: (0, 0)),
                  pl.BlockSpec((CF_K, w), lambda b, i: (0, 0)),
                  vec, vec, vec],
        out_specs=[tile, tile],
        out_shape=[jax.ShapeDtypeStruct((bn, n, w), BF16)] * 2,
        scratch_shapes=[pltpu.VMEM((n + 2 * CONV_HALO, w), F32)] * 2,
        compiler_params=_params("parallel", "arbitrary"),
        name="convs",
    )(v, gb, u, sc_w, cf_w, cf_b.reshape(1, -1), ln_g.reshape(1, -1), ln_b.reshape(1, -1))


def _split_mod(mod, bn):
    return [mod[:bn, j * D_MODEL:(j + 1) * D_MODEL].reshape(bn, 1, D_MODEL) for j in range(6)]


def _moe(x, gate2, aff_t, h2, w1, w3, w2, final_g, final):
    pos, gsel, post = _route(aff_t)
    xg, gs = _gather(pos, gsel, h2)
    y = _expert_ffn(xg, gs, w1, w3, w2)
    return _combine(post, y, x, gate2, final_g, final)


def kernel(x, c, ctx, c_ctx, l0_norm1, l0_w_mod, l0_b_mod, l0_w_in, l0_w_out, l0_rpb, l0_lam_q1, l0_lam_k1, l0_lam_q2, l0_lam_k2, l0_subln, l0_norm2, l0_w_router, l0_w1, l0_w3, l0_w2, l1_norm1, l1_w_mod, l1_b_mod, l1_w_in, l1_w_out, l1_sc_w, l1_cf_w, l1_cf_b, l1_ln_g, l1_ln_b, l1_norm2, l1_w_router, l1_w1, l1_w3, l1_w2, final_norm):
    bn = x.shape[0]
    pad_rows = -(bn + 1) % 8
    cc = jnp.concatenate([c, c_ctx[None, :], jnp.zeros((pad_rows, D_MODEL), F32)], axis=0)

    mod0 = _modulation(cc, l0_w_mod, l0_b_mod)
    shift1, scale1, gate1, shift2, scale2, gate2 = _split_mod(mod0, bn)
    ctx_shift = jnp.broadcast_to(mod0[bn, :D_MODEL].reshape(1, 1, D_MODEL), (bn, 1, D_MODEL))
    ctx_scale = jnp.broadcast_to(mod0[bn, D_MODEL:2 * D_MODEL].reshape(1, 1, D_MODEL), (bn, 1, D_MODEL))
    w_in0 = l0_w_in.astype(BF16)
    cos_t, sin_t = _rope_tables()
    z = _attn_proj(x, l0_norm1, shift1, scale1, w_in0, cos_t, sin_t, rope_groups=(1, 3), scaled_groups=(0, 1))
    zc = _attn_proj(ctx, l0_norm1, ctx_shift, ctx_scale, w_in0[:, 2 * HEAD_GROUP:], cos_t[:CTX_LEN], sin_t[:CTX_LEN],
                    rope_groups=(), scaled_groups=())
    y_a = _na_attention(z, zc, _na_bias_table(l0_rpb))
    y_b = _diff_attention(z, zc, l0_lam_q1, l0_lam_k1, l0_lam_q2, l0_lam_k2, l0_subln)
    x1, h2, aff_t = _out_proj(x, y_a, y_b, l0_w_out.astype(BF16), gate1, l0_norm2, shift2, scale2,
                              l0_w_router.T.astype(BF16))
    x2 = _moe(x1, gate2, aff_t, h2, l0_w1, l0_w3, l0_w2, final_norm, final=False)

    mod1 = _modulation(cc, l1_w_mod, l1_b_mod)
    shift1b, scale1b, gate1b, shift2b, scale2b, gate2b = _split_mod(mod1, bn)
    v, gb, u = _conv_proj(x2, l1_norm1, shift1b, scale1b, l1_w_in.astype(BF16))
    y_c, y_d = _convs(v, gb, u, l1_sc_w, l1_cf_w, l1_cf_b, l1_ln_g, l1_ln_b)
    x3, h2b, aff_tb = _out_proj(x2, y_c, y_d, l1_w_out.astype(BF16), gate1b, l1_norm2, shift2b, scale2b,
                                l1_w_router.T.astype(BF16))
    return _moe(x3, gate2b, aff_tb, h2b, l1_w1, l1_w3, l1_w2, final_norm, final=True)
```

```python
import functools
import math

import numpy as np
import jax
import jax.numpy as jnp
from jax import lax
from jax.experimental import pallas as pl
from jax.experimental.pallas import tpu as pltpu

D_MODEL = 1024
BATCH = 16
SEQ = 2048
GRID_W = 64
GRID_H = SEQ // GRID_W
CTX_LEN = 256
NA_HEADS = 8
NA_WIN_H = 8
NA_WIN_W = 16
DIFF_HEADS = 4
DIFF_DIM = 64
HEAD_GROUP = 512
ATTN_IN = 6 * HEAD_GROUP
ROPE_THETA = 10000.0
SC_K = 3
CF_K = 31
CONV_W = 512
CONV_IN = 5 * CONV_W
N_EXPERTS = 16
D_EXPERT = 2816
CAP = 2 * SEQ // N_EXPERTS
EPS = 1e-6
LAM_INIT0 = 0.8 - 0.6 * math.exp(-0.3 * 0)
NEG = -1e30
LOG2E = math.log2(math.e)
Q_SCALE = DIFF_DIM ** -0.5 * LOG2E

LANES = 128
NA_ROWS = 4
NA_BAND = 12
PROJ_TN = 512
DIFF_TQ = 256
CONV_TN = 256
CONV_HALO = 16
CONV_SUB = 128
ROUTE_B = 8
FFN_TF = 256
FFN_M_SPLIT = 2
FFN_ROW_CHUNKS = 2
COMBINE_TN = 512
VMEM_LIMIT = 60 * 1024 * 1024

F32 = jnp.float32
BF16 = jnp.bfloat16


def _dot(a, b):
    return jnp.dot(a, b, preferred_element_type=F32)


def _dot_nt(a, b):
    return lax.dot_general(a, b, (((1,), (1,)), ((), ())), preferred_element_type=F32)


def _dot_tn(a, b):
    return lax.dot_general(a, b, (((0,), (0,)), ((), ())), preferred_element_type=F32)


def _sigmoid(x):
    return 1.0 / (1.0 + jnp.exp(-x))


def _params(*sem):
    return pltpu.CompilerParams(dimension_semantics=sem, vmem_limit_bytes=VMEM_LIMIT)


def _norm_mod(x, g, shift, scale):
    ms = jnp.mean(x * x, axis=-1, keepdims=True)
    return (x * lax.rsqrt(ms + EPS)) * g * (1.0 + scale) + shift


def _mod_kernel(c_ref, w_ref, b_ref, o_ref):
    cc = c_ref[...]
    s = cc * _sigmoid(cc)
    o_ref[...] = _dot(s.astype(BF16), w_ref[...].astype(BF16)) + b_ref[...]


def _modulation(cc, w_mod, b_mod):
    rows = cc.shape[0]
    tc = D_MODEL
    return pl.pallas_call(
        _mod_kernel,
        grid=(w_mod.shape[1] // tc,),
        in_specs=[pl.BlockSpec((rows, D_MODEL), lambda j: (0, 0)),
                  pl.BlockSpec((D_MODEL, tc), lambda j: (0, j)),
                  pl.BlockSpec((1, tc), lambda j: (0, j))],
        out_specs=pl.BlockSpec((rows, tc), lambda j: (0, j)),
        out_shape=jax.ShapeDtypeStruct((rows, w_mod.shape[1]), F32),
        compiler_params=_params("arbitrary"),
        name="modulation",
    )(cc, w_mod, b_mod.reshape(1, -1))


def _rope(z, cos, sin):
    lane = lax.broadcasted_iota(jnp.int32, z.shape, 1)
    partner = jnp.where((lane & 31) < 16, pltpu.roll(z, LANES - 16, 1), pltpu.roll(z, 16, 1))
    return z * cos + partner * sin


def _attn_proj_kernel(x_ref, g_ref, sh_ref, sc_ref, w_ref, cos_ref, sin_ref, o_ref, *, rope_groups, scaled_groups):
    h = _norm_mod(x_ref[0], g_ref[...], sh_ref[0], sc_ref[0]).astype(BF16)
    n_groups = w_ref.shape[1] // HEAD_GROUP
    for j in range(n_groups):
        z = _dot(h, w_ref[:, j * HEAD_GROUP:(j + 1) * HEAD_GROUP])
        for s in range(HEAD_GROUP // LANES):
            zz = z[:, s * LANES:(s + 1) * LANES]
            if j in rope_groups:
                zz = _rope(zz, cos_ref[...], sin_ref[...])
            if j in scaled_groups:
                zz = zz * Q_SCALE
            o_ref[0, :, j * HEAD_GROUP + s * LANES:j * HEAD_GROUP + (s + 1) * LANES] = zz.astype(BF16)


def _attn_proj(x, g, shift, scale, w, cos_t, sin_t, rope_groups, scaled_groups):
    bn, n, _ = x.shape
    tn = min(PROJ_TN, n)
    cols = w.shape[1]
    return pl.pallas_call(
        functools.partial(_attn_proj_kernel, rope_groups=rope_groups, scaled_groups=scaled_groups),
        grid=(bn, n // tn),
        in_specs=[pl.BlockSpec((1, tn, D_MODEL), lambda b, i: (b, i, 0)),
                  pl.BlockSpec((1, D_MODEL), lambda b, i: (0, 0)),
                  pl.BlockSpec((1, 1, D_MODEL), lambda b, i: (b, 0, 0)),
                  pl.BlockSpec((1, 1, D_MODEL), lambda b, i: (b, 0, 0)),
                  pl.BlockSpec((D_MODEL, cols), lambda b, i: (0, 0)),
                  pl.BlockSpec((tn, LANES), lambda b, i: (i, 0)),
                  pl.BlockSpec((tn, LANES), lambda b, i: (i, 0))],
        out_specs=pl.BlockSpec((1, tn, cols), lambda b, i: (b, i, 0)),
        out_shape=jax.ShapeDtypeStruct((bn, n, cols), BF16),
        compiler_params=_params("parallel", "parallel"),
        name="attn_proj",
    )(x, g.reshape(1, -1), shift, scale, w, cos_t, sin_t)


def _rope_tables():
    n_freq = DIFF_DIM // 4
    freqs = ROPE_THETA ** (-jnp.arange(n_freq, dtype=F32) / n_freq)
    t = jnp.arange(SEQ)
    row, col = t // GRID_W, t % GRID_W

    def half(pos):
        ang = pos.astype(F32)[:, None] * freqs
        c, s = jnp.cos(ang), jnp.sin(ang)
        return jnp.concatenate([c, c], axis=-1), jnp.concatenate([-s, s], axis=-1)

    cr, sr = half(row)
    cc, sc = half(col)
    cos64 = jnp.concatenate([cr, cc], axis=-1)
    sin64 = jnp.concatenate([sr, sc], axis=-1)
    return jnp.tile(cos64, (1, LANES // DIFF_DIM)), jnp.tile(sin64, (1, LANES // DIFF_DIM))


def _na_bias_table(rpb):
    c = np.arange(GRID_W)
    cs = np.clip(c - NA_WIN_W // 2, 0, GRID_W - NA_WIN_W)
    kc = np.arange(GRID_W)
    col_ok = (kc[None, :] >= cs[:, None]) & (kc[None, :] < cs[:, None] + NA_WIN_W)
    dc = kc[None, :] - c[:, None] + NA_WIN_W - 1
    one_hot = (dc[None] == np.arange(2 * NA_WIN_W - 1)[:, None, None]) & col_ok[None]
    blocks = np.array([0, 2 * NA_ROWS, GRID_H - NA_ROWS])
    starts = np.clip(blocks - NA_WIN_H // 2, 0, GRID_H - NA_BAND)
    r = blocks[:, None] + np.arange(NA_ROWS)[None, :]
    rs = np.clip(r - NA_WIN_H // 2, 0, GRID_H - NA_WIN_H)
    krow = starts[:, None, None] + np.arange(NA_BAND)[None, None, :]
    row_ok = (krow >= rs[:, :, None]) & (krow < rs[:, :, None] + NA_WIN_H)
    dr = krow - r[:, :, None] + NA_WIN_H - 1
    row_hot = (dr[..., None] == np.arange(2 * NA_WIN_H - 1)) & row_ok[..., None]
    table = jnp.einsum('xqkr,hrd,dcj->xhqckj', jnp.asarray(row_hot, F32), rpb.astype(F32) * LOG2E,
                       jnp.asarray(one_hot, F32), precision=lax.Precision.HIGHEST)
    ok = jnp.asarray(row_ok[:, None, :, None, :, None] & col_ok[None, None, None, :, None, :])
    return jnp.where(ok, table, NEG).reshape(len(blocks), NA_HEADS, NA_ROWS * GRID_W, NA_BAND * GRID_W)


def _na_band_start(i):
    return jnp.clip(i * NA_ROWS - NA_WIN_H // 2, 0, GRID_H - NA_BAND)


def _na_kernel(q_ref, k_ref, v_ref, kc_ref, vc_ref, bias_ref, o_ref):
    start = pl.multiple_of(_na_band_start(pl.program_id(1)) * GRID_W, GRID_W)
    band = NA_BAND * GRID_W
    lane = lax.broadcasted_iota(jnp.int32, (q_ref.shape[1], LANES), 1)
    first = lane < (LANES // 2)
    for p in range(NA_HEADS // 2):
        cols = slice(p * LANES, (p + 1) * LANES)
        q2 = q_ref[0, :, cols]
        kb = k_ref[0, pl.ds(start, band), cols]
        vb = v_ref[0, pl.ds(start, band), cols]
        kc = kc_ref[0, :, cols]
        vc = vc_ref[0, :, cols]
        outs = []
        for par in range(2):
            qm = jnp.where(first if par == 0 else jnp.logical_not(first), q2, jnp.zeros_like(q2))
            s_w = _dot_nt(qm, kb) + bias_ref[0, 2 * p + par]
            s_c = _dot_nt(qm, kc)
            m = jnp.maximum(jnp.max(s_w, axis=-1, keepdims=True), jnp.max(s_c, axis=-1, keepdims=True))
            p_w = jnp.exp2(s_w - m)
            p_c = jnp.exp2(s_c - m)
            l = jnp.sum(p_w, axis=-1, keepdims=True) + jnp.sum(p_c, axis=-1, keepdims=True)
            o = _dot(p_w.astype(BF16), vb) + _dot(p_c.astype(BF16), vc)
            outs.append(o / l)
        o_ref[0, :, cols] = jnp.where(first, outs[0], outs[1]).astype(BF16)


def _na_block_class(i):
    return (i * NA_ROWS - _na_band_start(i)) // NA_ROWS


def _na_attention(z, zc, bias):
    bn = z.shape[0]
    tq = NA_ROWS * GRID_W
    return pl.pallas_call(
        _na_kernel,
        grid=(bn, GRID_H // NA_ROWS),
        in_specs=[pl.BlockSpec((1, tq, HEAD_GROUP), lambda b, i: (b, i, 0)),
                  pl.BlockSpec((1, SEQ, HEAD_GROUP), lambda b, i: (b, 0, 2)),
                  pl.BlockSpec((1, SEQ, HEAD_GROUP), lambda b, i: (b, 0, 4)),
                  pl.BlockSpec((1, CTX_LEN, HEAD_GROUP), lambda b, i: (b, 0, 0)),
                  pl.BlockSpec((1, CTX_LEN, HEAD_GROUP), lambda b, i: (b, 0, 2)),
                  pl.BlockSpec((1, NA_HEADS, tq, NA_BAND * GRID_W), lambda b, i: (_na_block_class(i), 0, 0, 0))],
        out_specs=pl.BlockSpec((1, tq, HEAD_GROUP), lambda b, i: (b, i, 0)),
        out_shape=jax.ShapeDtypeStruct((bn, SEQ, HEAD_GROUP), BF16),
        compiler_params=_params("parallel", "arbitrary"),
        name="na_attention",
    )(z, z, z, zc, zc, bias)


def _diff_kernel(lq1_ref, lk1_ref, lq2_ref, lk2_ref, sub_ref, q_ref, k_ref, v_ref, kc_ref, vc_ref, o_ref):
    lam = (jnp.exp(jnp.sum(lq1_ref[...] * lk1_ref[...], axis=-1, keepdims=True))
           - jnp.exp(jnp.sum(lq2_ref[...] * lk2_ref[...], axis=-1, keepdims=True)) + LAM_INIT0)
    tq = q_ref.shape[1]
    lane = lax.broadcasted_iota(jnp.int32, (tq, LANES), 1)
    first = lane < DIFF_DIM

    def scores(h):
        cols = slice(h * LANES, (h + 1) * LANES)
        q2 = q_ref[0, :, cols]
        out = []
        for mp in range(2):
            qm = jnp.where(first if mp == 0 else jnp.logical_not(first), q2, jnp.zeros_like(q2))
            out.append((_dot_nt(qm, k_ref[0, :, cols]), _dot_nt(qm, kc_ref[0, :, cols])))
        return out

    nxt = scores(0)
    for h in range(DIFF_HEADS):
        cols = slice(h * LANES, (h + 1) * LANES)
        cur = nxt
        if h + 1 < DIFF_HEADS:
            nxt = scores(h + 1)
        maps = []
        for s_l, s_c in cur:
            m = jnp.maximum(jnp.max(s_l, axis=-1, keepdims=True), jnp.max(s_c, axis=-1, keepdims=True))
            p_l = jnp.exp2(s_l - m)
            p_c = jnp.exp2(s_c - m)
            maps.append((p_l, p_c, jnp.sum(p_l, axis=-1, keepdims=True) + jnp.sum(p_c, axis=-1, keepdims=True)))
        ratio = lam * maps[0][2] / maps[1][2]
        a_l = (maps[0][0] - ratio * maps[1][0]).astype(BF16)
        a_c = (maps[0][1] - ratio * maps[1][1]).astype(BF16)
        o = (_dot(a_l, v_ref[0, :, cols]) + _dot(a_c, vc_ref[0, :, cols])) / maps[0][2]
        ms = jnp.mean(o * o, axis=-1, keepdims=True)
        o = (o * lax.rsqrt(ms + EPS)) * sub_ref[...] * (1.0 - LAM_INIT0)
        o_ref[0, :, cols] = o.astype(BF16)


def _diff_attention(z, zc, lq1, lk1, lq2, lk2, subln):
    bn = z.shape[0]
    tq = DIFF_TQ
    vec = pl.BlockSpec((1, DIFF_DIM), lambda b, i: (0, 0))
    return pl.pallas_call(
        _diff_kernel,
        grid=(bn, SEQ // tq),
        in_specs=[vec, vec, vec, vec,
                  pl.BlockSpec((1, 2 * DIFF_DIM), lambda b, i: (0, 0)),
                  pl.BlockSpec((1, tq, HEAD_GROUP), lambda b, i: (b, i, 1)),
                  pl.BlockSpec((1, SEQ, HEAD_GROUP), lambda b, i: (b, 0, 3)),
                  pl.BlockSpec((1, SEQ, HEAD_GROUP), lambda b, i: (b, 0, 5)),
                  pl.BlockSpec((1, CTX_LEN, HEAD_GROUP), lambda b, i: (b, 0, 1)),
                  pl.BlockSpec((1, CTX_LEN, HEAD_GROUP), lambda b, i: (b, 0, 3))],
        out_specs=pl.BlockSpec((1, tq, HEAD_GROUP), lambda b, i: (b, i, 0)),
        out_shape=jax.ShapeDtypeStruct((bn, SEQ, HEAD_GROUP), BF16),
        compiler_params=_params("parallel", "arbitrary"),
        name="diff_attention",
    )(lq1.reshape(1, -1), lk1.reshape(1, -1), lq2.reshape(1, -1), lk2.reshape(1, -1), subln.reshape(1, -1),
      z, z, z, zc, zc)


def _out_kernel(x_ref, ya_ref, yb_ref, w_ref, g1_ref, n2_ref, sh2_ref, sc2_ref, wr_ref, x1_ref, h2_ref, aff_ref):
    half = ya_ref.shape[2]
    y = _dot(ya_ref[0], w_ref[:half, :]) + _dot(yb_ref[0], w_ref[half:, :])
    x1 = x_ref[0] + g1_ref[0] * y
    x1_ref[0] = x1
    hb = _norm_mod(x1, n2_ref[...], sh2_ref[0], sc2_ref[0]).astype(BF16)
    h2_ref[0] = hb
    logits = _dot_nt(wr_ref[...], hb)
    m = jnp.max(logits, axis=0, keepdims=True)
    p = jnp.exp(logits - m)
    aff_ref[0] = p / jnp.sum(p, axis=0, keepdims=True)


def _out_proj(x, ya, yb, w_out, gate1, norm2, shift2, scale2, w_router_t):
    bn, n, _ = x.shape
    tn = PROJ_TN
    half = ya.shape[2]
    tok = lambda b, i: (b, i, 0)
    per_b = lambda b, i: (b, 0, 0)
    return pl.pallas_call(
        _out_kernel,
        grid=(bn, n // tn),
        in_specs=[pl.BlockSpec((1, tn, D_MODEL), tok),
                  pl.BlockSpec((1, tn, half), tok),
                  pl.BlockSpec((1, tn, half), tok),
                  pl.BlockSpec((2 * half, D_MODEL), lambda b, i: (0, 0)),
                  pl.BlockSpec((1, 1, D_MODEL), per_b),
                  pl.BlockSpec((1, D_MODEL), lambda b, i: (0, 0)),
                  pl.BlockSpec((1, 1, D_MODEL), per_b),
                  pl.BlockSpec((1, 1, D_MODEL), per_b),
                  pl.BlockSpec((N_EXPERTS, D_MODEL), lambda b, i: (0, 0))],
        out_specs=[pl.BlockSpec((1, tn, D_MODEL), tok),
                   pl.BlockSpec((1, tn, D_MODEL), tok),
                   pl.BlockSpec((1, N_EXPERTS, tn), lambda b, i: (b, 0, i))],
        out_shape=[jax.ShapeDtypeStruct((bn, n, D_MODEL), F32),
                   jax.ShapeDtypeStruct((bn, n, D_MODEL), BF16),
                   jax.ShapeDtypeStruct((bn, N_EXPERTS, n), F32)],
        compiler_params=_params("parallel", "parallel"),
        name="out_proj",
    )(x, ya, yb, w_out, gate1, norm2.reshape(1, -1), shift2, scale2, w_router_t)


def _excl_cumsum_lanes(x01):
    e, n = x01.shape
    i = lax.broadcasted_iota(jnp.int32, (LANES, LANES), 0)
    j = lax.broadcasted_iota(jnp.int32, (LANES, LANES), 1)
    tri = jnp.where(i < j, 1.0, 0.0).astype(BF16)
    offset = jnp.zeros((e, 1), F32)
    chunks = []
    for c in range(n // LANES):
        xc = x01[:, c * LANES:(c + 1) * LANES]
        chunks.append(_dot(xc.astype(BF16), tri) + offset)
        offset = offset + jnp.sum(xc, axis=-1, keepdims=True)
    return jnp.concatenate(chunks, axis=-1), offset


def _route_kernel(aff_ref, pos_ref, g_ref, post_ref):
    nb, ne, n = aff_ref.shape
    a = aff_ref[...].reshape(nb * ne, n)
    thr = jnp.zeros((a.shape[0], 1), jnp.int32)
    for bit in range(30, -1, -1):
        cand = thr | (1 << bit)
        cnt = jnp.sum(jnp.where(a >= lax.bitcast_convert_type(cand, F32), 1.0, 0.0), axis=-1, keepdims=True)
        thr = jnp.where(cnt >= CAP, cand, thr)
    gt = jnp.where(a >= lax.bitcast_convert_type(thr + 1, F32), 1.0, 0.0)
    eq = jnp.where(a >= lax.bitcast_convert_type(thr, F32), 1.0, 0.0) - gt
    need = CAP - jnp.sum(gt, axis=-1, keepdims=True)
    eq_rank, _ = _excl_cumsum_lanes(eq)
    sel = gt + eq * jnp.where(eq_rank < need, 1.0, 0.0)
    pos, _ = _excl_cumsum_lanes(sel)
    pos = jnp.where(sel > 0.5, pos, -1.0)
    pos_ref[...] = pos.astype(jnp.int32).reshape(nb, ne, n)
    g_ref[...] = jnp.where(sel > 0.5, a, 0.0).reshape(nb, ne, n)
    fill = jnp.full((LANES - ne, n), -1.0, F32)
    for b in range(nb):
        padded = jnp.concatenate([pos[b * ne:(b + 1) * ne, :], fill], axis=0)
        post_ref[b] = padded.T.astype(jnp.int32)


def _route(aff_t):
    bn, e, n = aff_t.shape
    blk = pl.BlockSpec((ROUTE_B, e, n), lambda b: (b, 0, 0))
    return pl.pallas_call(
        _route_kernel,
        grid=(bn // ROUTE_B,),
        in_specs=[blk],
        out_specs=[blk, blk, pl.BlockSpec((ROUTE_B, n, LANES), lambda b: (b, 0, 0))],
        out_shape=[jax.ShapeDtypeStruct((bn, e, n), jnp.int32), jax.ShapeDtypeStruct((bn, e, n), F32),
                   jax.ShapeDtypeStruct((bn, n, LANES), jnp.int32)],
        compiler_params=_params("parallel"),
        name="route",
    )(aff_t)


def _one_hot_slots(pos_row):
    slot = lax.broadcasted_iota(jnp.int32, (CAP, pos_row.shape[1]), 0)
    return slot == pos_row


def _gather_kernel(pos_ref, g_ref, h_ref, x_ref, gs_ref):
    for e in range(N_EXPERTS):
        hit = _one_hot_slots(pos_ref[0, e:e + 1, :])
        x_ref[e] = _dot(jnp.where(hit, 1.0, 0.0).astype(BF16), h_ref[0]).astype(BF16)
        g = jnp.sum(jnp.where(hit, g_ref[0, e:e + 1, :], 0.0), axis=-1, keepdims=True)
        gs_ref[e] = jnp.broadcast_to(g, (CAP, LANES))


def _gather(pos, gsel, h2):
    bn, e, n = pos.shape
    rows = pl.BlockSpec((1, e, n), lambda b: (b, 0, 0))
    return pl.pallas_call(
        _gather_kernel,
        grid=(bn,),
        in_specs=[rows, rows, pl.BlockSpec((1, n, D_MODEL), lambda b: (b, 0, 0))],
        out_specs=[pl.BlockSpec((e, CAP, D_MODEL), lambda b: (0, b, 0)),
                   pl.BlockSpec((e, CAP, LANES), lambda b: (0, b, 0))],
        out_shape=[jax.ShapeDtypeStruct((e, bn * CAP, D_MODEL), BF16),
                   jax.ShapeDtypeStruct((e, bn * CAP, LANES), F32)],
        compiler_params=_params("parallel"),
        name="moe_gather",
    )(pos, gsel, h2)


def _ffn_kernel(x_ref, gs_ref, w1_ref, w3_ref, w2_ref, y_ref, acc_ref):
    f = pl.program_id(2)
    n_f = pl.num_programs(2)
    @pl.when(f == 0)
    def _first():
        acc_ref[...] = jnp.zeros_like(acc_ref)

    w1 = w1_ref[0].astype(BF16)
    w3 = w3_ref[0].astype(BF16)
    w2 = w2_ref[0].astype(BF16)
    chunk = x_ref.shape[1] // FFN_ROW_CHUNKS
    for c in range(FFN_ROW_CHUNKS):
        rows = slice(c * chunk, (c + 1) * chunk)
        xs = x_ref[0, rows, :]
        a = _dot(xs, w1)
        hid = (a * _sigmoid(a)) * _dot(xs, w3)
        acc_ref[rows, :] += _dot(hid.astype(BF16), w2)

    @pl.when(f == n_f - 1)
    def _emit():
        g = gs_ref[0]
        for j in range(D_MODEL // LANES):
            cols = slice(j * LANES, (j + 1) * LANES)
            y_ref[0, :, cols] = (acc_ref[:, cols] * g).astype(BF16)


def _expert_ffn(xg, gs, w1, w3, w2):
    e, rows, _ = xg.shape
    tm = rows // FFN_M_SPLIT
    n_f = D_EXPERT // FFN_TF
    return pl.pallas_call(
        _ffn_kernel,
        grid=(e, FFN_M_SPLIT, n_f),
        in_specs=[pl.BlockSpec((1, tm, D_MODEL), lambda ex, m, f: (ex, m, 0)),
                  pl.BlockSpec((1, tm, LANES), lambda ex, m, f: (ex, m, 0)),
                  pl.BlockSpec((1, D_MODEL, FFN_TF), lambda ex, m, f: (ex, 0, f)),
                  pl.BlockSpec((1, D_MODEL, FFN_TF), lambda ex, m, f: (ex, 0, f)),
                  pl.BlockSpec((1, FFN_TF, D_MODEL), lambda ex, m, f: (ex, f, 0))],
        out_specs=pl.BlockSpec((1, tm, D_MODEL), lambda ex, m, f: (ex, m, 0)),
        out_shape=jax.ShapeDtypeStruct((e, rows, D_MODEL), BF16),
        scratch_shapes=[pltpu.VMEM((tm, D_MODEL), F32)],
        compiler_params=_params("parallel", "parallel", "arbitrary"),
        name="expert_ffn",
    )(xg, gs, w1, w3, w2)


def _combine_kernel(post_ref, y_ref, x_ref, g2_ref, fin_ref, o_ref, *, final):
    tn = post_ref.shape[1]
    slot = lax.broadcasted_iota(jnp.int32, (tn, CAP), 1)
    hits = [jnp.where(post_ref[0, :, e:e + 1] == slot, 1.0, 0.0).astype(BF16) for e in range(N_EXPERTS)]
    moe = _dot(jnp.concatenate(hits, axis=-1), y_ref[...].reshape(N_EXPERTS * CAP, D_MODEL))
    x = x_ref[0] + g2_ref[0] * moe
    if final:
        ms = jnp.mean(x * x, axis=-1, keepdims=True)
        x = (x * lax.rsqrt(ms + EPS)) * fin_ref[...]
    o_ref[0] = x


def _combine(post, y, x, gate2, final_g, final):
    bn, n, _ = x.shape
    tn = COMBINE_TN
    tok = lambda b, i: (b, i, 0)
    return pl.pallas_call(
        functools.partial(_combine_kernel, final=final),
        grid=(bn, n // tn),
        in_specs=[pl.BlockSpec((1, tn, LANES), tok),
                  pl.BlockSpec((N_EXPERTS, CAP, D_MODEL), lambda b, i: (0, b, 0)),
                  pl.BlockSpec((1, tn, D_MODEL), tok),
                  pl.BlockSpec((1, 1, D_MODEL), lambda b, i: (b, 0, 0)),
                  pl.BlockSpec((1, D_MODEL), lambda b, i: (0, 0))],
        out_specs=pl.BlockSpec((1, tn, D_MODEL), tok),
        out_shape=jax.ShapeDtypeStruct((bn, n, D_MODEL), F32),
        compiler_params=_params("parallel", "arbitrary"),
        name="moe_combine",
    )(post, y, x, gate2, final_g.reshape(1, -1))


def _conv_proj_kernel(x_ref, g_ref, sh_ref, sc_ref, w_ref, v_ref, gb_ref, u_ref):
    h = _norm_mod(x_ref[0], g_ref[...], sh_ref[0], sc_ref[0]).astype(BF16)
    w = CONV_W
    xc = _dot(h, w_ref[:, 0 * w:1 * w])
    gate_c = _dot(h, w_ref[:, 2 * w:3 * w])
    v_ref[0] = gate_c * xc
    gb_ref[0] = _dot(h, w_ref[:, 1 * w:2 * w])
    glu_a = _dot(h, w_ref[:, 3 * w:4 * w])
    glu_g = _dot(h, w_ref[:, 4 * w:5 * w])
    u_ref[0] = glu_a * _sigmoid(glu_g)


def _conv_proj(x, g, shift, scale, w):
    bn, n, _ = x.shape
    tn = PROJ_TN
    tok = lambda b, i: (b, i, 0)
    per_b = lambda b, i: (b, 0, 0)
    narrow = pl.BlockSpec((1, tn, CONV_W), tok)
    vec = pl.BlockSpec((1, 1, D_MODEL), per_b)
    return pl.pallas_call(
        _conv_proj_kernel,
        grid=(bn, n // tn),
        in_specs=[pl.BlockSpec((1, tn, D_MODEL), tok),
                  pl.BlockSpec((1, D_MODEL), lambda b, i: (0, 0)),
                  vec, vec,
                  pl.BlockSpec((D_MODEL, CONV_IN), lambda b, i: (0, 0))],
        out_specs=[narrow, narrow, narrow],
        out_shape=[jax.ShapeDtypeStruct((bn, n, CONV_W), F32)] * 3,
        compiler_params=_params("parallel", "parallel"),
        name="conv_proj",
    )(x, g.reshape(1, -1), shift, scale, w)


def _depthwise(win, w_ref, cols, taps, rows):
    offs = [CONV_HALO + k - taps // 2 for k in range(taps)]
    acc = None
    for b in range(8):
        ks = [k for k in range(taps) if offs[k] % 8 == b]
        if not ks:
            continue
        shifted = pltpu.roll(win, win.shape[0] - b, 0) if b else win
        for k in ks:
            term = shifted[offs[k] - b:offs[k] - b + rows, :] * w_ref[k:k + 1, cols]
            acc = term if acc is None else acc + term
    return acc


def _conv_kernel(v_ref, gb_ref, u_ref, scw_ref, cfw_ref, cfb_ref, lng_ref, lnb_ref, yc_ref, yd_ref, vpad_ref, upad_ref):
    i = pl.program_id(1)
    n = v_ref.shape[1]
    tn = gb_ref.shape[1]

    @pl.when(i == 0)
    def _stage():
        zeros = jnp.zeros((CONV_HALO, CONV_W), F32)
        for pad_ref, src_ref in ((vpad_ref, v_ref), (upad_ref, u_ref)):
            pad_ref[0:CONV_HALO, :] = zeros
            pad_ref[CONV_HALO:CONV_HALO + n, :] = src_ref[0]
            pad_ref[CONV_HALO + n:, :] = zeros

    for j in range(tn // CONV_SUB):
        rows = slice(j * CONV_SUB, (j + 1) * CONV_SUB)
        start = pl.multiple_of(i * tn + j * CONV_SUB, CONV_SUB)
        yd_parts = []
        for s in range(CONV_W // LANES):
            cols = slice(s * LANES, (s + 1) * LANES)
            vwin = vpad_ref[pl.ds(start, CONV_SUB + 2 * CONV_HALO), cols]
            yc = gb_ref[0, rows, cols] * _depthwise(vwin, scw_ref, cols, SC_K, CONV_SUB)
            yc_ref[0, rows, cols] = yc.astype(BF16)
            uwin = upad_ref[pl.ds(start, CONV_SUB + 2 * CONV_HALO), cols]
            yd_parts.append(_depthwise(uwin, cfw_ref, cols, CF_K, CONV_SUB) + cfb_ref[:, cols])
        u = jnp.concatenate(yd_parts, axis=-1)
        mu = jnp.mean(u, axis=-1, keepdims=True)
        d = u - mu
        var = jnp.mean(d * d, axis=-1, keepdims=True)
        ln = (d * lax.rsqrt(var + EPS)) * lng_ref[...] + lnb_ref[...]
        yd_ref[0, rows, :] = (ln * _sigmoid(ln)).astype(BF16)


def _convs(v, gb, u, sc_w, cf_w, cf_b, ln_g, ln_b):
    bn, n, w = v.shape
    tn = CONV_TN
    full = pl.BlockSpec((1, n, w), lambda b, i: (b, 0, 0))
    tile = pl.BlockSpec((1, tn, w), lambda b, i: (b, i, 0))
    vec = pl.BlockSpec((1, w), lambda b, i: (0, 0))
    return pl.pallas_call(
        _conv_kernel,
        grid=(bn, n // tn),
        in_specs=[full, tile, full,
                  pl.BlockSpec((SC_K, w), lambda b, i: (0, 0)),
                  pl.BlockSpec((CF_K, w), lambda b, i: (0, 0)),
                  vec, vec, vec],
        out_specs=[tile, tile],
        out_shape=[jax.ShapeDtypeStruct((bn, n, w), BF16)] * 2,
        scratch_shapes=[pltpu.VMEM((n + 2 * CONV_HALO, w), F32)] * 2,
        compiler_params=_params("parallel", "arbitrary"),
        name="convs",
    )(v, gb, u, sc_w, cf_w, cf_b.reshape(1, -1), ln_g.reshape(1, -1), ln_b.reshape(1, -1))


def _split_mod(mod, bn):
    return [mod[:bn, j * D_MODEL:(j + 1) * D_MODEL].reshape(bn, 1, D_MODEL) for j in range(6)]


def _moe(x, gate2, aff_t, h2, w1, w3, w2, final_g, final):
    pos, gsel, post = _route(aff_t)
    xg, gs = _gather(pos, gsel, h2)
    y = _expert_ffn(xg, gs, w1, w3, w2)
    return _combine(post, y, x, gate2, final_g, final)


def kernel(x, c, ctx, c_ctx, l0_norm1, l0_w_mod, l0_b_mod, l0_w_in, l0_w_out, l0_rpb, l0_lam_q1, l0_lam_k1, l0_lam_q2, l0_lam_k2, l0_subln, l0_norm2, l0_w_router, l0_w1, l0_w3, l0_w2, l1_norm1, l1_w_mod, l1_b_mod, l1_w_in, l1_w_out, l1_sc_w, l1_cf_w, l1_cf_b, l1_ln_g, l1_ln_b, l1_norm2, l1_w_router, l1_w1, l1_w3, l1_w2, final_norm):
    bn = x.shape[0]
    pad_rows = -(bn + 1) % 8
    cc = jnp.concatenate([c, c_ctx[None, :], jnp.zeros((pad_rows, D_MODEL), F32)], axis=0)

    mod0 = _modulation(cc, l0_w_mod, l0_b_mod)
    shift1, scale1, gate1, shift2, scale2, gate2 = _split_mod(mod0, bn)
    ctx_shift = jnp.broadcast_to(mod0[bn, :D_MODEL].reshape(1, 1, D_MODEL), (bn, 1, D_MODEL))
    ctx_scale = jnp.broadcast_to(mod0[bn, D_MODEL:2 * D_MODEL].reshape(1, 1, D_MODEL), (bn, 1, D_MODEL))
    w_in0 = l0_w_in.astype(BF16)
    cos_t, sin_t = _rope_tables()
    z = _attn_proj(x, l0_norm1, shift1, scale1, w_in0, cos_t, sin_t, rope_groups=(1, 3), scaled_groups=(0, 1))
    zc = _attn_proj(ctx, l0_norm1, ctx_shift, ctx_scale, w_in0[:, 2 * HEAD_GROUP:], cos_t[:CTX_LEN], sin_t[:CTX_LEN],
                    rope_groups=(), scaled_groups=())
    y_a = _na_attention(z, zc, _na_bias_table(l0_rpb))
    y_b = _diff_attention(z, zc, l0_lam_q1, l0_lam_k1, l0_lam_q2, l0_lam_k2, l0_subln)
    x1, h2, aff_t = _out_proj(x, y_a, y_b, l0_w_out.astype(BF16), gate1, l0_norm2, shift2, scale2,
                              l0_w_router.T.astype(BF16))
    x2 = _moe(x1, gate2, aff_t, h2, l0_w1, l0_w3, l0_w2, final_norm, final=False)

    mod1 = _modulation(cc, l1_w_mod, l1_b_mod)
    shift1b, scale1b, gate1b, shift2b, scale2b, gate2b = _split_mod(mod1, bn)
    v, gb, u = _conv_proj(x2, l1_norm1, shift1b, scale1b, l1_w_in.astype(BF16))
    y_c, y_d = _convs(v, gb, u, l1_sc_w, l1_cf_w, l1_cf_b, l1_ln_g, l1_ln_b)
    x3, h2b, aff_tb = _out_proj(x2, y_c, y_d, l1_w_out.astype(BF16), gate1b, l1_norm2, shift2b, scale2b,
                                l1_w_router.T.astype(BF16))
    return _moe(x3, gate2b, aff_tb, h2b, l1_w1, l1_w3, l1_w2, final_norm, final=True)
```

```python
import functools
import math

import numpy as np
import jax
import jax.numpy as jnp
from jax import lax
from jax.experimental import pallas as pl
from jax.experimental.pallas import tpu as pltpu

D_MODEL = 1024
BATCH = 16
SEQ = 2048
GRID_W = 64
GRID_H = SEQ // GRID_W
CTX_LEN = 256
NA_HEADS = 8
NA_WIN_H = 8
NA_WIN_W = 16
DIFF_HEADS = 4
DIFF_DIM = 64
HEAD_GROUP = 512
ATTN_IN = 6 * HEAD_GROUP
ROPE_THETA = 10000.0
SC_K = 3
CF_K = 31
CONV_W = 512
CONV_IN = 5 * CONV_W
N_EXPERTS = 16
D_EXPERT = 2816
CAP = 2 * SEQ // N_EXPERTS
EPS = 1e-6
LAM_INIT0 = 0.8 - 0.6 * math.exp(-0.3 * 0)
NEG = -1e30
LOG2E = math.log2(math.e)
Q_SCALE = DIFF_DIM ** -0.5 * LOG2E

LANES = 128
NA_ROWS = 4
NA_BAND = 12
PROJ_TN = 1024
DIFF_TQ = 512
CONV_TN = 512
CONV_HALO = 16
CONV_SUB = 128
ROUTE_B = 8
FFN_TF = 256
FFN_M_SPLIT = 2
FFN_ROW_CHUNKS = 2
COMBINE_TN = 1024
VMEM_LIMIT = 60 * 1024 * 1024

F32 = jnp.float32
BF16 = jnp.bfloat16


def _dot(a, b):
    return jnp.dot(a, b, preferred_element_type=F32)


def _dot_nt(a, b):
    return lax.dot_general(a, b, (((1,), (1,)), ((), ())), preferred_element_type=F32)


def _dot_tn(a, b):
    return lax.dot_general(a, b, (((0,), (0,)), ((), ())), preferred_element_type=F32)


def _sigmoid(x):
    return 1.0 / (1.0 + jnp.exp(-x))


def _params(*sem):
    return pltpu.CompilerParams(dimension_semantics=sem, vmem_limit_bytes=VMEM_LIMIT)


def _norm_mod(x, g, shift, scale):
    ms = jnp.mean(x * x, axis=-1, keepdims=True)
    return (x * lax.rsqrt(ms + EPS)) * g * (1.0 + scale) + shift


def _mod_kernel(c_ref, w_ref, b_ref, o_ref):
    cc = c_ref[...]
    s = cc * _sigmoid(cc)
    o_ref[...] = _dot(s.astype(BF16), w_ref[...].astype(BF16)) + b_ref[...]


def _modulation(cc, w_mod, b_mod):
    rows = cc.shape[0]
    tc = D_MODEL
    return pl.pallas_call(
        _mod_kernel,
        grid=(w_mod.shape[1] // tc,),
        in_specs=[pl.BlockSpec((rows, D_MODEL), lambda j: (0, 0)),
                  pl.BlockSpec((D_MODEL, tc), lambda j: (0, j)),
                  pl.BlockSpec((1, tc), lambda j: (0, j))],
        out_specs=pl.BlockSpec((rows, tc), lambda j: (0, j)),
        out_shape=jax.ShapeDtypeStruct((rows, w_mod.shape[1]), F32),
        compiler_params=_params("arbitrary"),
        name="modulation",
    )(cc, w_mod, b_mod.reshape(1, -1))


def _rope(z, cos, sin):
    lane = lax.broadcasted_iota(jnp.int32, z.shape, 1)
    partner = jnp.where((lane & 31) < 16, pltpu.roll(z, LANES - 16, 1), pltpu.roll(z, 16, 1))
    return z * cos + partner * sin


def _attn_proj_kernel(x_ref, g_ref, sh_ref, sc_ref, w_ref, cos_ref, sin_ref, o_ref, *, rope_groups, scaled_groups):
    h = _norm_mod(x_ref[0], g_ref[...], sh_ref[0], sc_ref[0]).astype(BF16)
    n_groups = w_ref.shape[1] // HEAD_GROUP
    for j in range(n_groups):
        z = _dot(h, w_ref[:, j * HEAD_GROUP:(j + 1) * HEAD_GROUP])
        for s in range(HEAD_GROUP // LANES):
            zz = z[:, s * LANES:(s + 1) * LANES]
            if j in rope_groups:
                zz = _rope(zz, cos_ref[...], sin_ref[...])
            if j in scaled_groups:
                zz = zz * Q_SCALE
            o_ref[0, :, j * HEAD_GROUP + s * LANES:j * HEAD_GROUP + (s + 1) * LANES] = zz.astype(BF16)


def _attn_proj(x, g, shift, scale, w, cos_t, sin_t, rope_groups, scaled_groups):
    bn, n, _ = x.shape
    tn = min(PROJ_TN, n)
    cols = w.shape[1]
    return pl.pallas_call(
        functools.partial(_attn_proj_kernel, rope_groups=rope_groups, scaled_groups=scaled_groups),
        grid=(bn, n // tn),
        in_specs=[pl.BlockSpec((1, tn, D_MODEL), lambda b, i: (b, i, 0)),
                  pl.BlockSpec((1, D_MODEL), lambda b, i: (0, 0)),
                  pl.BlockSpec((1, 1, D_MODEL), lambda b, i: (b, 0, 0)),
                  pl.BlockSpec((1, 1, D_MODEL), lambda b, i: (b, 0, 0)),
                  pl.BlockSpec((D_MODEL, cols), lambda b, i: (0, 0)),
                  pl.BlockSpec((tn, LANES), lambda b, i: (i, 0)),
                  pl.BlockSpec((tn, LANES), lambda b, i: (i, 0))],
        out_specs=pl.BlockSpec((1, tn, cols), lambda b, i: (b, i, 0)),
        out_shape=jax.ShapeDtypeStruct((bn, n, cols), BF16),
        compiler_params=_params("parallel", "parallel"),
        name="attn_proj",
    )(x, g.reshape(1, -1), shift, scale, w, cos_t, sin_t)


def _rope_tables():
    n_freq = DIFF_DIM // 4
    freqs = ROPE_THETA ** (-jnp.arange(n_freq, dtype=F32) / n_freq)
    t = jnp.arange(SEQ)
    row, col = t // GRID_W, t % GRID_W

    def half(pos):
        ang = pos.astype(F32)[:, None] * freqs
        c, s = jnp.cos(ang), jnp.sin(ang)
        return jnp.concatenate([c, c], axis=-1), jnp.concatenate([-s, s], axis=-1)

    cr, sr = half(row)
    cc, sc = half(col)
    cos64 = jnp.concatenate([cr, cc], axis=-1)
    sin64 = jnp.concatenate([sr, sc], axis=-1)
    return jnp.tile(cos64, (1, LANES // DIFF_DIM)), jnp.tile(sin64, (1, LANES // DIFF_DIM))


def _na_bias_table(rpb):
    n_dr, n_dc = 2 * NA_WIN_H - 1, 2 * NA_WIN_W - 1
    c = np.arange(GRID_W)
    cs = np.clip(c - NA_WIN_W // 2, 0, GRID_W - NA_WIN_W)
    kc = np.arange(GRID_W)
    col_ok = (kc[None, :] >= cs[:, None]) & (kc[None, :] < cs[:, None] + NA_WIN_W)
    dc = kc[None, :] - c[:, None] + NA_WIN_W - 1
    one_hot = (dc[None] == np.arange(n_dc)[:, None, None]) & col_ok[None]
    pair_hot = np.zeros((2, n_dc, GRID_W, 2, GRID_W), np.float32)
    for k2 in range(2):
        pair_hot[k2, :, :, k2, :] = one_hot
    pair_hot = pair_hot.reshape(2 * n_dc, GRID_W, 2 * GRID_W)
    padded = jnp.pad(rpb.astype(F32) * LOG2E, ((0, 0), (1, 1), (0, 0)))
    pairs = jnp.concatenate([padded[:, :-1], padded[:, 1:]], axis=-1)
    slabs = jnp.einsum('hsd,dcn->hscn', pairs, jnp.asarray(pair_hot), precision=lax.Precision.HIGHEST)
    blocks = np.array([0, 2 * NA_ROWS, GRID_H - NA_ROWS])
    starts = np.clip(blocks - NA_WIN_H // 2, 0, GRID_H - NA_BAND)
    r = blocks[:, None] + np.arange(NA_ROWS)[None, :]
    rs = np.clip(r - NA_WIN_H // 2, 0, GRID_H - NA_WIN_H)
    krow = starts[:, None, None] + np.arange(NA_BAND)[None, None, :]
    row_ok = (krow >= rs[:, :, None]) & (krow < rs[:, :, None] + NA_WIN_H)
    dr_even = krow[:, :, 0::2] - r[:, :, None] + NA_WIN_H - 1
    slab_idx = np.clip(dr_even + 1, 0, n_dr)
    picked = jnp.take(slabs, jnp.asarray(slab_idx.reshape(-1)), axis=1)
    picked = picked.reshape(NA_HEADS, len(blocks), NA_ROWS, NA_BAND // 2, GRID_W, 2 * GRID_W)
    row_ok_lanes = np.repeat(row_ok.reshape(len(blocks), NA_ROWS, NA_BAND // 2, 2), GRID_W, axis=-1)
    ok = row_ok_lanes[None, :, :, :, None, :] & np.tile(col_ok, (1, 2))[None, None, None, None, :, :]
    table = jnp.where(jnp.asarray(ok), picked, NEG).transpose(1, 0, 3, 2, 4, 5)
    return table.reshape(len(blocks), NA_HEADS, NA_BAND // 2, NA_ROWS * GRID_W, 2 * GRID_W)


def _na_band_start(i):
    return jnp.clip(i * NA_ROWS - NA_WIN_H // 2, 0, GRID_H - NA_BAND)


def _na_kernel(q_ref, k_ref, v_ref, kc_ref, vc_ref, bias_ref, o_ref):
    start = pl.multiple_of(_na_band_start(pl.program_id(1)) * GRID_W, GRID_W)
    band = NA_BAND * GRID_W
    lane = lax.broadcasted_iota(jnp.int32, (q_ref.shape[1], LANES), 1)
    first = lane < (LANES // 2)
    for p in range(NA_HEADS // 2):
        cols = slice(p * LANES, (p + 1) * LANES)
        q2 = q_ref[0, :, cols]
        kb = k_ref[0, pl.ds(start, band), cols]
        vb = v_ref[0, pl.ds(start, band), cols]
        kc = kc_ref[0, :, cols]
        vc = vc_ref[0, :, cols]
        outs = []
        for par in range(2):
            qm = jnp.where(first if par == 0 else jnp.logical_not(first), q2, jnp.zeros_like(q2))
            s_w = _dot_nt(qm, kb)
            s_w = jnp.concatenate([s_w[:, m * LANES:(m + 1) * LANES] + bias_ref[0, 2 * p + par, m]
                                   for m in range(NA_BAND // 2)], axis=-1)
            s_c = _dot_nt(qm, kc)
            m = jnp.maximum(jnp.max(s_w, axis=-1, keepdims=True), jnp.max(s_c, axis=-1, keepdims=True))
            p_w = jnp.exp2(s_w - m)
            p_c = jnp.exp2(s_c - m)
            l = jnp.sum(p_w, axis=-1, keepdims=True) + jnp.sum(p_c, axis=-1, keepdims=True)
            o = _dot(p_w.astype(BF16), vb) + _dot(p_c.astype(BF16), vc)
            outs.append(o / l)
        o_ref[0, :, cols] = jnp.where(first, outs[0], outs[1]).astype(BF16)


def _na_block_class(i):
    return (i * NA_ROWS - _na_band_start(i)) // NA_ROWS


def _na_attention(z, zc, bias):
    bn = z.shape[0]
    tq = NA_ROWS * GRID_W
    return pl.pallas_call(
        _na_kernel,
        grid=(bn, GRID_H // NA_ROWS),
        in_specs=[pl.BlockSpec((1, tq, HEAD_GROUP), lambda b, i: (b, i, 0)),
                  pl.BlockSpec((1, SEQ, HEAD_GROUP), lambda b, i: (b, 0, 2)),
                  pl.BlockSpec((1, SEQ, HEAD_GROUP), lambda b, i: (b, 0, 4)),
                  pl.BlockSpec((1, CTX_LEN, HEAD_GROUP), lambda b, i: (b, 0, 0)),
                  pl.BlockSpec((1, CTX_LEN, HEAD_GROUP), lambda b, i: (b, 0, 2)),
                  pl.BlockSpec((1, NA_HEADS, NA_BAND // 2, tq, 2 * GRID_W),
                               lambda b, i: (_na_block_class(i), 0, 0, 0, 0))],
        out_specs=pl.BlockSpec((1, tq, HEAD_GROUP), lambda b, i: (b, i, 0)),
        out_shape=jax.ShapeDtypeStruct((bn, SEQ, HEAD_GROUP), BF16),
        compiler_params=_params("parallel", "arbitrary"),
        name="na_attention",
    )(z, z, z, zc, zc, bias)


def _diff_kernel(lq1_ref, lk1_ref, lq2_ref, lk2_ref, sub_ref, q_ref, k_ref, v_ref, kc_ref, vc_ref, o_ref):
    lam = (jnp.exp(jnp.sum(lq1_ref[...] * lk1_ref[...], axis=-1, keepdims=True))
           - jnp.exp(jnp.sum(lq2_ref[...] * lk2_ref[...], axis=-1, keepdims=True)) + LAM_INIT0)
    tq = q_ref.shape[1]
    lane = lax.broadcasted_iota(jnp.int32, (tq, LANES), 1)
    first = lane < DIFF_DIM

    def scores(h):
        cols = slice(h * LANES, (h + 1) * LANES)
        q2 = q_ref[0, :, cols]
        out = []
        for mp in range(2):
            qm = jnp.where(first if mp == 0 else jnp.logical_not(first), q2, jnp.zeros_like(q2))
            out.append((_dot_nt(qm, k_ref[0, :, cols]), _dot_nt(qm, kc_ref[0, :, cols])))
        return out

    nxt = scores(0)
    for h in range(DIFF_HEADS):
        cols = slice(h * LANES, (h + 1) * LANES)
        cur = nxt
        if h + 1 < DIFF_HEADS:
            nxt = scores(h + 1)
        maps = []
        for s_l, s_c in cur:
            m = jnp.maximum(jnp.max(s_l, axis=-1, keepdims=True), jnp.max(s_c, axis=-1, keepdims=True))
            p_l = jnp.exp2(s_l - m)
            p_c = jnp.exp2(s_c - m)
            maps.append((p_l, p_c, jnp.sum(p_l, axis=-1, keepdims=True) + jnp.sum(p_c, axis=-1, keepdims=True)))
        ratio = lam * maps[0][2] / maps[1][2]
        a_l = (maps[0][0] - ratio * maps[1][0]).astype(BF16)
        a_c = (maps[0][1] - ratio * maps[1][1]).astype(BF16)
        o = (_dot(a_l, v_ref[0, :, cols]) + _dot(a_c, vc_ref[0, :, cols])) / maps[0][2]
        ms = jnp.mean(o * o, axis=-1, keepdims=True)
        o = (o * lax.rsqrt(ms + EPS)) * sub_ref[...] * (1.0 - LAM_INIT0)
        o_ref[0, :, cols] = o.astype(BF16)


def _diff_attention(z, zc, lq1, lk1, lq2, lk2, subln):
    bn = z.shape[0]
    tq = DIFF_TQ
    vec = pl.BlockSpec((1, DIFF_DIM), lambda b, i: (0, 0))
    return pl.pallas_call(
        _diff_kernel,
        grid=(bn, SEQ // tq),
        in_specs=[vec, vec, vec, vec,
                  pl.BlockSpec((1, 2 * DIFF_DIM), lambda b, i: (0, 0)),
                  pl.BlockSpec((1, tq, HEAD_GROUP), lambda b, i: (b, i, 1)),
                  pl.BlockSpec((1, SEQ, HEAD_GROUP), lambda b, i: (b, 0, 3)),
                  pl.BlockSpec((1, SEQ, HEAD_GROUP), lambda b, i: (b, 0, 5)),
                  pl.BlockSpec((1, CTX_LEN, HEAD_GROUP), lambda b, i: (b, 0, 1)),
                  pl.BlockSpec((1, CTX_LEN, HEAD_GROUP), lambda b, i: (b, 0, 3))],
        out_specs=pl.BlockSpec((1, tq, HEAD_GROUP), lambda b, i: (b, i, 0)),
        out_shape=jax.ShapeDtypeStruct((bn, SEQ, HEAD_GROUP), BF16),
        compiler_params=_params("parallel", "arbitrary"),
        name="diff_attention",
    )(lq1.reshape(1, -1), lk1.reshape(1, -1), lq2.reshape(1, -1), lk2.reshape(1, -1), subln.reshape(1, -1),
      z, z, z, zc, zc)


def _out_kernel(x_ref, ya_ref, yb_ref, w_ref, g1_ref, n2_ref, sh2_ref, sc2_ref, wr_ref, x1_ref, h2_ref, aff_ref):
    half = ya_ref.shape[2]
    y = _dot(ya_ref[0], w_ref[:half, :]) + _dot(yb_ref[0], w_ref[half:, :])
    x1 = x_ref[0] + g1_ref[0] * y
    x1_ref[0] = x1
    hb = _norm_mod(x1, n2_ref[...], sh2_ref[0], sc2_ref[0]).astype(BF16)
    h2_ref[0] = hb
    logits = _dot_nt(wr_ref[...], hb)
    m = jnp.max(logits, axis=0, keepdims=True)
    p = jnp.exp(logits - m)
    aff_ref[0] = p / jnp.sum(p, axis=0, keepdims=True)


def _out_proj(x, ya, yb, w_out, gate1, norm2, shift2, scale2, w_router_t):
    bn, n, _ = x.shape
    tn = PROJ_TN
    half = ya.shape[2]
    tok = lambda b, i: (b, i, 0)
    per_b = lambda b, i: (b, 0, 0)
    return pl.pallas_call(
        _out_kernel,
        grid=(bn, n // tn),
        in_specs=[pl.BlockSpec((1, tn, D_MODEL), tok),
                  pl.BlockSpec((1, tn, half), tok),
                  pl.BlockSpec((1, tn, half), tok),
                  pl.BlockSpec((2 * half, D_MODEL), lambda b, i: (0, 0)),
                  pl.BlockSpec((1, 1, D_MODEL), per_b),
                  pl.BlockSpec((1, D_MODEL), lambda b, i: (0, 0)),
                  pl.BlockSpec((1, 1, D_MODEL), per_b),
                  pl.BlockSpec((1, 1, D_MODEL), per_b),
                  pl.BlockSpec((N_EXPERTS, D_MODEL), lambda b, i: (0, 0))],
        out_specs=[pl.BlockSpec((1, tn, D_MODEL), tok),
                   pl.BlockSpec((1, tn, D_MODEL), tok),
                   pl.BlockSpec((1, N_EXPERTS, tn), lambda b, i: (b, 0, i))],
        out_shape=[jax.ShapeDtypeStruct((bn, n, D_MODEL), F32),
                   jax.ShapeDtypeStruct((bn, n, D_MODEL), BF16),
                   jax.ShapeDtypeStruct((bn, N_EXPERTS, n), F32)],
        compiler_params=_params("parallel", "parallel"),
        name="out_proj",
    )(x, ya, yb, w_out, gate1, norm2.reshape(1, -1), shift2, scale2, w_router_t)


def _excl_cumsum_lanes(x01):
    e, n = x01.shape
    i = lax.broadcasted_iota(jnp.int32, (LANES, LANES), 0)
    j = lax.broadcasted_iota(jnp.int32, (LANES, LANES), 1)
    tri = jnp.where(i < j, 1.0, 0.0).astype(BF16)
    offset = jnp.zeros((e, 1), F32)
    chunks = []
    for c in range(n // LANES):
        xc = x01[:, c * LANES:(c + 1) * LANES]
        chunks.append(_dot(xc.astype(BF16), tri) + offset)
        offset = offset + jnp.sum(xc, axis=-1, keepdims=True)
    return jnp.concatenate(chunks, axis=-1), offset


def _route_kernel(aff_ref, pos_ref, g_ref, post_ref):
    nb, ne, n = aff_ref.shape
    a = aff_ref[...].reshape(nb * ne, n)
    thr = jnp.zeros((a.shape[0], 1), jnp.int32)
    for bit in range(30, -1, -1):
        cand = thr | (1 << bit)
        cnt = jnp.sum(jnp.where(a >= lax.bitcast_convert_type(cand, F32), 1.0, 0.0), axis=-1, keepdims=True)
        thr = jnp.where(cnt >= CAP, cand, thr)
    gt = jnp.where(a >= lax.bitcast_convert_type(thr + 1, F32), 1.0, 0.0)
    eq = jnp.where(a >= lax.bitcast_convert_type(thr, F32), 1.0, 0.0) - gt
    need = CAP - jnp.sum(gt, axis=-1, keepdims=True)
    eq_rank, _ = _excl_cumsum_lanes(eq)
    sel = gt + eq * jnp.where(eq_rank < need, 1.0, 0.0)
    pos, _ = _excl_cumsum_lanes(sel)
    pos = jnp.where(sel > 0.5, pos, -1.0)
    pos_ref[...] = pos.astype(jnp.int32).reshape(nb, ne, n)
    g_ref[...] = jnp.where(sel > 0.5, a, 0.0).reshape(nb, ne, n)
    fill = jnp.full((LANES - ne, n), -1.0, F32)
    for b in range(nb):
        padded = jnp.concatenate([pos[b * ne:(b + 1) * ne, :], fill], axis=0)
        post_ref[b] = padded.T.astype(jnp.int32)


def _route(aff_t):
    bn, e, n = aff_t.shape
    blk = pl.BlockSpec((ROUTE_B, e, n), lambda b: (b, 0, 0))
    return pl.pallas_call(
        _route_kernel,
        grid=(bn // ROUTE_B,),
        in_specs=[blk],
        out_specs=[blk, blk, pl.BlockSpec((ROUTE_B, n, LANES), lambda b: (b, 0, 0))],
        out_shape=[jax.ShapeDtypeStruct((bn, e, n), jnp.int32), jax.ShapeDtypeStruct((bn, e, n), F32),
                   jax.ShapeDtypeStruct((bn, n, LANES), jnp.int32)],
        compiler_params=_params("parallel"),
        name="route",
    )(aff_t)


def _one_hot_slots(pos_row):
    slot = lax.broadcasted_iota(jnp.int32, (CAP, pos_row.shape[1]), 0)
    return slot == pos_row


def _gather_kernel(pos_ref, g_ref, h_ref, x_ref, gs_ref):
    for e in range(N_EXPERTS):
        hit = _one_hot_slots(pos_ref[0, e:e + 1, :])
        x_ref[e] = _dot(jnp.where(hit, 1.0, 0.0).astype(BF16), h_ref[0]).astype(BF16)
        g = jnp.sum(jnp.where(hit, g_ref[0, e:e + 1, :], 0.0), axis=-1, keepdims=True)
        gs_ref[e] = jnp.broadcast_to(g, (CAP, LANES))


def _gather(pos, gsel, h2):
    bn, e, n = pos.shape
    rows = pl.BlockSpec((1, e, n), lambda b: (b, 0, 0))
    return pl.pallas_call(
        _gather_kernel,
        grid=(bn,),
        in_specs=[rows, rows, pl.BlockSpec((1, n, D_MODEL), lambda b: (b, 0, 0))],
        out_specs=[pl.BlockSpec((e, CAP, D_MODEL), lambda b: (0, b, 0)),
                   pl.BlockSpec((e, CAP, LANES), lambda b: (0, b, 0))],
        out_shape=[jax.ShapeDtypeStruct((e, bn * CAP, D_MODEL), BF16),
                   jax.ShapeDtypeStruct((e, bn * CAP, LANES), F32)],
        compiler_params=_params("parallel"),
        name="moe_gather",
    )(pos, gsel, h2)


def _ffn_kernel(x_ref, gs_ref, w1_ref, w3_ref, w2_ref, y_ref, acc_ref):
    f = pl.program_id(2)
    n_f = pl.num_programs(2)
    @pl.when(f == 0)
    def _first():
        acc_ref[...] = jnp.zeros_like(acc_ref)

    w1 = w1_ref[0].astype(BF16)
    w3 = w3_ref[0].astype(BF16)
    w2 = w2_ref[0].astype(BF16)
    chunk = x_ref.shape[1] // FFN_ROW_CHUNKS
    for c in range(FFN_ROW_CHUNKS):
        rows = slice(c * chunk, (c + 1) * chunk)
        xs = x_ref[0, rows, :]
        a = _dot(xs, w1)
        hid = (a * _sigmoid(a)) * _dot(xs, w3)
        acc_ref[rows, :] += _dot(hid.astype(BF16), w2)

    @pl.when(f == n_f - 1)
    def _emit():
        g = gs_ref[0]
        for j in range(D_MODEL // LANES):
            cols = slice(j * LANES, (j + 1) * LANES)
            y_ref[0, :, cols] = (acc_ref[:, cols] * g).astype(BF16)


def _expert_ffn(xg, gs, w1, w3, w2):
    e, rows, _ = xg.shape
    tm = rows // FFN_M_SPLIT
    n_f = D_EXPERT // FFN_TF
    return pl.pallas_call(
        _ffn_kernel,
        grid=(e, FFN_M_SPLIT, n_f),
        in_specs=[pl.BlockSpec((1, tm, D_MODEL), lambda ex, m, f: (ex, m, 0)),
                  pl.BlockSpec((1, tm, LANES), lambda ex, m, f: (ex, m, 0)),
                  pl.BlockSpec((1, D_MODEL, FFN_TF), lambda ex, m, f: (ex, 0, f)),
                  pl.BlockSpec((1, D_MODEL, FFN_TF), lambda ex, m, f: (ex, 0, f)),
                  pl.BlockSpec((1, FFN_TF, D_MODEL), lambda ex, m, f: (ex, f, 0))],
        out_specs=pl.BlockSpec((1, tm, D_MODEL), lambda ex, m, f: (ex, m, 0)),
        out_shape=jax.ShapeDtypeStruct((e, rows, D_MODEL), BF16),
        scratch_shapes=[pltpu.VMEM((tm, D_MODEL), F32)],
        compiler_params=_params("parallel", "parallel", "arbitrary"),
        name="expert_ffn",
    )(xg, gs, w1, w3, w2)


def _combine_kernel(post_ref, y_ref, x_ref, g2_ref, fin_ref, o_ref, *, final):
    tn = post_ref.shape[1]
    slot = lax.broadcasted_iota(jnp.int32, (tn, CAP), 1)
    hits = [jnp.where(post_ref[0, :, e:e + 1] == slot, 1.0, 0.0).astype(BF16) for e in range(N_EXPERTS)]
    moe = _dot(jnp.concatenate(hits, axis=-1), y_ref[...].reshape(N_EXPERTS * CAP, D_MODEL))
    x = x_ref[0] + g2_ref[0] * moe
    if final:
        ms = jnp.mean(x * x, axis=-1, keepdims=True)
        x = (x * lax.rsqrt(ms + EPS)) * fin_ref[...]
    o_ref[0] = x


def _combine(post, y, x, gate2, final_g, final):
    bn, n, _ = x.shape
    tn = COMBINE_TN
    tok = lambda b, i: (b, i, 0)
    return pl.pallas_call(
        functools.partial(_combine_kernel, final=final),
        grid=(bn, n // tn),
        in_specs=[pl.BlockSpec((1, tn, LANES), tok),
                  pl.BlockSpec((N_EXPERTS, CAP, D_MODEL), lambda b, i: (0, b, 0)),
                  pl.BlockSpec((1, tn, D_MODEL), tok),
                  pl.BlockSpec((1, 1, D_MODEL), lambda b, i: (b, 0, 0)),
                  pl.BlockSpec((1, D_MODEL), lambda b, i: (0, 0))],
        out_specs=pl.BlockSpec((1, tn, D_MODEL), tok),
        out_shape=jax.ShapeDtypeStruct((bn, n, D_MODEL), F32),
        compiler_params=_params("parallel", "arbitrary"),
        name="moe_combine",
    )(post, y, x, gate2, final_g.reshape(1, -1))


def _conv_proj_kernel(x_ref, g_ref, sh_ref, sc_ref, w_ref, v_ref, gb_ref, u_ref):
    h = _norm_mod(x_ref[0], g_ref[...], sh_ref[0], sc_ref[0]).astype(BF16)
    w = CONV_W
    xc = _dot(h, w_ref[:, 0 * w:1 * w])
    gate_c = _dot(h, w_ref[:, 2 * w:3 * w])
    v_ref[0] = gate_c * xc
    gb_ref[0] = _dot(h, w_ref[:, 1 * w:2 * w])
    glu_a = _dot(h, w_ref[:, 3 * w:4 * w])
    glu_g = _dot(h, w_ref[:, 4 * w:5 * w])
    u_ref[0] = glu_a * _sigmoid(glu_g)


def _conv_proj(x, g, shift, scale, w):
    bn, n, _ = x.shape
    tn = PROJ_TN
    tok = lambda b, i: (b, i, 0)
    per_b = lambda b, i: (b, 0, 0)
    narrow = pl.BlockSpec((1, tn, CONV_W), tok)
    vec = pl.BlockSpec((1, 1, D_MODEL), per_b)
    return pl.pallas_call(
        _conv_proj_kernel,
        grid=(bn, n // tn),
        in_specs=[pl.BlockSpec((1, tn, D_MODEL), tok),
                  pl.BlockSpec((1, D_MODEL), lambda b, i: (0, 0)),
                  vec, vec,
                  pl.BlockSpec((D_MODEL, CONV_IN), lambda b, i: (0, 0))],
        out_specs=[narrow, narrow, narrow],
        out_shape=[jax.ShapeDtypeStruct((bn, n, CONV_W), F32)] * 3,
        compiler_params=_params("parallel", "parallel"),
        name="conv_proj",
    )(x, g.reshape(1, -1), shift, scale, w)


def _depthwise(win, w_ref, cols, taps, rows):
    offs = [CONV_HALO + k - taps // 2 for k in range(taps)]
    acc = None
    for b in range(8):
        ks = [k for k in range(taps) if offs[k] % 8 == b]
        if not ks:
            continue
        shifted = pltpu.roll(win, win.shape[0] - b, 0) if b else win
        for k in ks:
            term = shifted[offs[k] - b:offs[k] - b + rows, :] * w_ref[k:k + 1, cols]
            acc = term if acc is None else acc + term
    return acc


def _conv_kernel(v_ref, gb_ref, u_ref, scw_ref, cfw_ref, cfb_ref, lng_ref, lnb_ref, yc_ref, yd_ref, vpad_ref, upad_ref):
    i = pl.program_id(1)
    n = v_ref.shape[1]
    tn = gb_ref.shape[1]

    @pl.when(i == 0)
    def _stage():
        zeros = jnp.zeros((CONV_HALO, CONV_W), F32)
        for pad_ref, src_ref in ((vpad_ref, v_ref), (upad_ref, u_ref)):
            pad_ref[0:CONV_HALO, :] = zeros
            pad_ref[CONV_HALO:CONV_HALO + n, :] = src_ref[0]
            pad_ref[CONV_HALO + n:, :] = zeros

    for j in range(tn // CONV_SUB):
        rows = slice(j * CONV_SUB, (j + 1) * CONV_SUB)
        start = pl.multiple_of(i * tn + j * CONV_SUB, CONV_SUB)
        yd_parts = []
        for s in range(CONV_W // LANES):
            cols = slice(s * LANES, (s + 1) * LANES)
            vwin = vpad_ref[pl.ds(start, CONV_SUB + 2 * CONV_HALO), cols]
            yc = gb_ref[0, rows, cols] * _depthwise(vwin, scw_ref, cols, SC_K, CONV_SUB)
            yc_ref[0, rows, cols] = yc.astype(BF16)
            uwin = upad_ref[pl.ds(start, CONV_SUB + 2 * CONV_HALO), cols]
            yd_parts.append(_depthwise(uwin, cfw_ref, cols, CF_K, CONV_SUB) + cfb_ref[:, cols])
        u = jnp.concatenate(yd_parts, axis=-1)
        mu = jnp.mean(u, axis=-1, keepdims=True)
        d = u - mu
        var = jnp.mean(d * d, axis=-1, keepdims=True)
        ln = (d * lax.rsqrt(var + EPS)) * lng_ref[...] + lnb_ref[...]
        yd_ref[0, rows, :] = (ln * _sigmoid(ln)).astype(BF16)


def _convs(v, gb, u, sc_w, cf_w, cf_b, ln_g, ln_b):
    bn, n, w = v.shape
    tn = CONV_TN
    full = pl.BlockSpec((1, n, w), lambda b, i: (b, 0, 0))
    tile = pl.BlockSpec((1, tn, w), lambda b, i: (b, i, 0))
    vec = pl.BlockSpec((1, w), lambda b, i: (0, 0))
    return pl.pallas_call(
        _conv_kernel,
        grid=(bn, n // tn),
        in_specs=[full, tile, full,
                  pl.BlockSpec((SC_K, w), lambda b, i: (0, 0)),
                  pl.BlockSpec((CF_K, w), lambda b, i: (0, 0)),
                  vec, vec, vec],
        out_specs=[tile, tile],
        out_shape=[jax.ShapeDtypeStruct((bn, n, w), BF16)] * 2,
        scratch_shapes=[pltpu.VMEM((n + 2 * CONV_HALO, w), F32)] * 2,
        compiler_params=_params("parallel", "arbitrary"),
        name="convs",
    )(v, gb, u, sc_w, cf_w, cf_b.reshape(1, -1), ln_g.reshape(1, -1), ln_b.reshape(1, -1))


def _split_mod(mod, bn):
    return [mod[:bn, j * D_MODEL:(j + 1) * D_MODEL].reshape(bn, 1, D_MODEL) for j in range(6)]


def _moe(x, gate2, aff_t, h2, w1, w3, w2, final_g, final):
    pos, gsel, post = _route(aff_t)
    xg, gs = _gather(pos, gsel, h2)
    y = _expert_ffn(xg, gs, w1, w3, w2)
    return _combine(post, y, x, gate2, final_g, final)


def kernel(x, c, ctx, c_ctx, l0_norm1, l0_w_mod, l0_b_mod, l0_w_in, l0_w_out, l0_rpb, l0_lam_q1, l0_lam_k1, l0_lam_q2, l0_lam_k2, l0_subln, l0_norm2, l0_w_router, l0_w1, l0_w3, l0_w2, l1_norm1, l1_w_mod, l1_b_mod, l1_w_in, l1_w_out, l1_sc_w, l1_cf_w, l1_cf_b, l1_ln_g, l1_ln_b, l1_norm2, l1_w_router, l1_w1, l1_w3, l1_w2, final_norm):
    bn = x.shape[0]
    pad_rows = -(bn + 1) % 8
    cc = jnp.concatenate([c, c_ctx[None, :], jnp.zeros((pad_rows, D_MODEL), F32)], axis=0)

    mod0 = _modulation(cc, l0_w_mod, l0_b_mod)
    shift1, scale1, gate1, shift2, scale2, gate2 = _split_mod(mod0, bn)
    ctx_shift = jnp.broadcast_to(mod0[bn, :D_MODEL].reshape(1, 1, D_MODEL), (bn, 1, D_MODEL))
    ctx_scale = jnp.broadcast_to(mod0[bn, D_MODEL:2 * D_MODEL].reshape(1, 1, D_MODEL), (bn, 1, D_MODEL))
    w_in0 = l0_w_in.astype(BF16)
    cos_t, sin_t = _rope_tables()
    z = _attn_proj(x, l0_norm1, shift1, scale1, w_in0, cos_t, sin_t, rope_groups=(1, 3), scaled_groups=(0, 1))
    zc = _attn_proj(ctx, l0_norm1, ctx_shift, ctx_scale, w_in0[:, 2 * HEAD_GROUP:], cos_t[:CTX_LEN], sin_t[:CTX_LEN],
                    rope_groups=(), scaled_groups=())
    y_a = _na_attention(z, zc, _na_bias_table(l0_rpb))
    y_b = _diff_attention(z, zc, l0_lam_q1, l0_lam_k1, l0_lam_q2, l0_lam_k2, l0_subln)
    x1, h2, aff_t = _out_proj(x, y_a, y_b, l0_w_out.astype(BF16), gate1, l0_norm2, shift2, scale2,
                              l0_w_router.T.astype(BF16))
    x2 = _moe(x1, gate2, aff_t, h2, l0_w1, l0_w3, l0_w2, final_norm, final=False)

    mod1 = _modulation(cc, l1_w_mod, l1_b_mod)
    shift1b, scale1b, gate1b, shift2b, scale2b, gate2b = _split_mod(mod1, bn)
    v, gb, u = _conv_proj(x2, l1_norm1, shift1b, scale1b, l1_w_in.astype(BF16))
    y_c, y_d = _convs(v, gb, u, l1_sc_w, l1_cf_w, l1_cf_b, l1_ln_g, l1_ln_b)
    x3, h2b, aff_tb = _out_proj(x2, y_c, y_d, l1_w_out.astype(BF16), gate1b, l1_norm2, shift2b, scale2b,
                                l1_w_router.T.astype(BF16))
    return _moe(x3, gate2b, aff_tb, h2b, l1_w1, l1_w3, l1_w2, final_norm, final=True)
```

```python
import functools
import math

import numpy as np
import jax
import jax.numpy as jnp
from jax import lax
from jax.experimental import pallas as pl
from jax.experimental.pallas import tpu as pltpu

D_MODEL = 1024
BATCH = 16
SEQ = 2048
GRID_W = 64
GRID_H = SEQ // GRID_W
CTX_LEN = 256
NA_HEADS = 8
NA_WIN_H = 8
NA_WIN_W = 16
DIFF_HEADS = 4
DIFF_DIM = 64
HEAD_GROUP = 512
ATTN_IN = 6 * HEAD_GROUP
ROPE_THETA = 10000.0
SC_K = 3
CF_K = 31
CONV_W = 512
CONV_IN = 5 * CONV_W
N_EXPERTS = 16
D_EXPERT = 2816
CAP = 2 * SEQ // N_EXPERTS
EPS = 1e-6
LAM_INIT0 = 0.8 - 0.6 * math.exp(-0.3 * 0)
NEG = -1e30
LOG2E = math.log2(math.e)
Q_SCALE = DIFF_DIM ** -0.5 * LOG2E

LANES = 128
NA_ROWS = 4
NA_BAND = 12
PROJ_TN = 1024
DIFF_TQ = 512
CONV_TN = 512
CONV_HALO = 16
CONV_SUB = 128
ROUTE_B = 8
FFN_TF = 256
FFN_M_SPLIT = 2
FFN_ROW_CHUNKS = 2
COMBINE_TN = 1024
VMEM_LIMIT = 60 * 1024 * 1024

F32 = jnp.float32
BF16 = jnp.bfloat16


def _dot(a, b):
    return jnp.dot(a, b, preferred_element_type=F32)


def _dot_nt(a, b):
    return lax.dot_general(a, b, (((1,), (1,)), ((), ())), preferred_element_type=F32)


def _dot_tn(a, b):
    return lax.dot_general(a, b, (((0,), (0,)), ((), ())), preferred_element_type=F32)


def _sigmoid(x):
    return 1.0 / (1.0 + jnp.exp(-x))


def _params(*sem):
    return pltpu.CompilerParams(dimension_semantics=sem, vmem_limit_bytes=VMEM_LIMIT)


def _norm_mod(x, g, shift, scale):
    ms = jnp.mean(x * x, axis=-1, keepdims=True)
    return (x * lax.rsqrt(ms + EPS)) * g * (1.0 + scale) + shift


def _mod_kernel(c_ref, w_ref, b_ref, o_ref):
    cc = c_ref[...]
    s = cc * _sigmoid(cc)
    o_ref[...] = _dot(s.astype(BF16), w_ref[...].astype(BF16)) + b_ref[...]


def _modulation(cc, w_mod, b_mod):
    rows = cc.shape[0]
    tc = D_MODEL
    return pl.pallas_call(
        _mod_kernel,
        grid=(w_mod.shape[1] // tc,),
        in_specs=[pl.BlockSpec((rows, D_MODEL), lambda j: (0, 0)),
                  pl.BlockSpec((D_MODEL, tc), lambda j: (0, j)),
                  pl.BlockSpec((1, tc), lambda j: (0, j))],
        out_specs=pl.BlockSpec((rows, tc), lambda j: (0, j)),
        out_shape=jax.ShapeDtypeStruct((rows, w_mod.shape[1]), F32),
        compiler_params=_params("arbitrary"),
        name="modulation",
    )(cc, w_mod, b_mod.reshape(1, -1))


def _rope(z, cos, sin):
    lane = lax.broadcasted_iota(jnp.int32, z.shape, 1)
    partner = jnp.where((lane & 31) < 16, pltpu.roll(z, LANES - 16, 1), pltpu.roll(z, 16, 1))
    return z * cos + partner * sin


def _attn_proj_kernel(x_ref, g_ref, sh_ref, sc_ref, w_ref, cos_ref, sin_ref, o_ref, *, rope_groups, scaled_groups):
    h = _norm_mod(x_ref[0], g_ref[...], sh_ref[0], sc_ref[0]).astype(BF16)
    n_groups = w_ref.shape[1] // HEAD_GROUP
    for j in range(n_groups):
        z = _dot(h, w_ref[:, j * HEAD_GROUP:(j + 1) * HEAD_GROUP])
        for s in range(HEAD_GROUP // LANES):
            zz = z[:, s * LANES:(s + 1) * LANES]
            if j in rope_groups:
                zz = _rope(zz, cos_ref[...], sin_ref[...])
            if j in scaled_groups:
                zz = zz * Q_SCALE
            o_ref[0, :, j * HEAD_GROUP + s * LANES:j * HEAD_GROUP + (s + 1) * LANES] = zz.astype(BF16)


def _attn_proj(x, g, shift, scale, w, cos_t, sin_t, rope_groups, scaled_groups):
    bn, n, _ = x.shape
    tn = min(PROJ_TN, n)
    cols = w.shape[1]
    return pl.pallas_call(
        functools.partial(_attn_proj_kernel, rope_groups=rope_groups, scaled_groups=scaled_groups),
        grid=(bn, n // tn),
        in_specs=[pl.BlockSpec((1, tn, D_MODEL), lambda b, i: (b, i, 0)),
                  pl.BlockSpec((1, D_MODEL), lambda b, i: (0, 0)),
                  pl.BlockSpec((1, 1, D_MODEL), lambda b, i: (b, 0, 0)),
                  pl.BlockSpec((1, 1, D_MODEL), lambda b, i: (b, 0, 0)),
                  pl.BlockSpec((D_MODEL, cols), lambda b, i: (0, 0)),
                  pl.BlockSpec((tn, LANES), lambda b, i: (i, 0)),
                  pl.BlockSpec((tn, LANES), lambda b, i: (i, 0))],
        out_specs=pl.BlockSpec((1, tn, cols), lambda b, i: (b, i, 0)),
        out_shape=jax.ShapeDtypeStruct((bn, n, cols), BF16),
        compiler_params=_params("parallel", "parallel"),
        name="attn_proj",
    )(x, g.reshape(1, -1), shift, scale, w, cos_t, sin_t)


def _rope_tables():
    n_freq = DIFF_DIM // 4
    freqs = ROPE_THETA ** (-jnp.arange(n_freq, dtype=F32) / n_freq)
    t = jnp.arange(SEQ)
    row, col = t // GRID_W, t % GRID_W

    def half(pos):
        ang = pos.astype(F32)[:, None] * freqs
        c, s = jnp.cos(ang), jnp.sin(ang)
        return jnp.concatenate([c, c], axis=-1), jnp.concatenate([-s, s], axis=-1)

    cr, sr = half(row)
    cc, sc = half(col)
    cos64 = jnp.concatenate([cr, cc], axis=-1)
    sin64 = jnp.concatenate([sr, sc], axis=-1)
    return jnp.tile(cos64, (1, LANES // DIFF_DIM)), jnp.tile(sin64, (1, LANES // DIFF_DIM))


def _na_bias_table(rpb):
    n_dr, n_dc = 2 * NA_WIN_H - 1, 2 * NA_WIN_W - 1
    c = np.arange(GRID_W)
    cs = np.clip(c - NA_WIN_W // 2, 0, GRID_W - NA_WIN_W)
    kc = np.arange(GRID_W)
    col_ok = (kc[None, :] >= cs[:, None]) & (kc[None, :] < cs[:, None] + NA_WIN_W)
    dc = kc[None, :] - c[:, None] + NA_WIN_W - 1
    one_hot = (dc[None] == np.arange(n_dc)[:, None, None]) & col_ok[None]
    pair_hot = np.zeros((2, n_dc, GRID_W, 2, GRID_W), np.float32)
    for k2 in range(2):
        pair_hot[k2, :, :, k2, :] = one_hot
    pair_hot = pair_hot.reshape(2 * n_dc, GRID_W, 2 * GRID_W)
    padded = jnp.pad(rpb.astype(F32) * LOG2E, ((0, 0), (1, 1), (0, 0)))
    pairs = jnp.concatenate([padded[:, :-1], padded[:, 1:]], axis=-1)
    slabs = jnp.einsum('hsd,dcn->hscn', pairs, jnp.asarray(pair_hot), precision=lax.Precision.HIGHEST)
    blocks = np.array([0, 2 * NA_ROWS, GRID_H - NA_ROWS])
    starts = np.clip(blocks - NA_WIN_H // 2, 0, GRID_H - NA_BAND)
    r = blocks[:, None] + np.arange(NA_ROWS)[None, :]
    rs = np.clip(r - NA_WIN_H // 2, 0, GRID_H - NA_WIN_H)
    krow = starts[:, None, None] + np.arange(NA_BAND)[None, None, :]
    row_ok = (krow >= rs[:, :, None]) & (krow < rs[:, :, None] + NA_WIN_H)
    dr_even = krow[:, :, 0::2] - r[:, :, None] + NA_WIN_H - 1
    slab_idx = np.clip(dr_even + 1, 0, n_dr)
    picked = jnp.take(slabs, jnp.asarray(slab_idx.reshape(-1)), axis=1)
    picked = picked.reshape(NA_HEADS, len(blocks), NA_ROWS, NA_BAND // 2, GRID_W, 2 * GRID_W)
    row_ok_lanes = np.repeat(row_ok.reshape(len(blocks), NA_ROWS, NA_BAND // 2, 2), GRID_W, axis=-1)
    ok = row_ok_lanes[None, :, :, :, None, :] & np.tile(col_ok, (1, 2))[None, None, None, None, :, :]
    table = jnp.where(jnp.asarray(ok), picked, NEG).transpose(1, 0, 3, 2, 4, 5)
    return table.reshape(len(blocks), NA_HEADS, NA_BAND // 2, NA_ROWS * GRID_W, 2 * GRID_W)


def _na_band_start(i):
    return jnp.clip(i * NA_ROWS - NA_WIN_H // 2, 0, GRID_H - NA_BAND)


def _na_kernel(q_ref, k_ref, v_ref, kc_ref, vc_ref, bias_ref, o_ref):
    start = pl.multiple_of(_na_band_start(pl.program_id(1)) * GRID_W, GRID_W)
    band = NA_BAND * GRID_W
    lane = lax.broadcasted_iota(jnp.int32, (q_ref.shape[1], LANES), 1)
    first = lane < (LANES // 2)
    for p in range(NA_HEADS // 2):
        cols = slice(p * LANES, (p + 1) * LANES)
        q2 = q_ref[0, :, cols]
        kb = k_ref[0, pl.ds(start, band), cols]
        vb = v_ref[0, pl.ds(start, band), cols]
        kc = kc_ref[0, :, cols]
        vc = vc_ref[0, :, cols]
        outs = []
        for par in range(2):
            qm = jnp.where(first if par == 0 else jnp.logical_not(first), q2, jnp.zeros_like(q2))
            s_w = _dot_nt(qm, kb)
            s_w = jnp.concatenate([s_w[:, m * LANES:(m + 1) * LANES] + bias_ref[0, 2 * p + par, m]
                                   for m in range(NA_BAND // 2)], axis=-1)
            s_c = _dot_nt(qm, kc)
            m = jnp.maximum(jnp.max(s_w, axis=-1, keepdims=True), jnp.max(s_c, axis=-1, keepdims=True))
            p_w = jnp.exp2(s_w - m)
            p_c = jnp.exp2(s_c - m)
            l = jnp.sum(p_w, axis=-1, keepdims=True) + jnp.sum(p_c, axis=-1, keepdims=True)
            o = _dot(p_w.astype(BF16), vb) + _dot(p_c.astype(BF16), vc)
            outs.append(o / l)
        o_ref[0, :, cols] = jnp.where(first, outs[0], outs[1]).astype(BF16)


def _na_block_class(i):
    return (i * NA_ROWS - _na_band_start(i)) // NA_ROWS


def _na_attention(z, zc, bias):
    bn = z.shape[0]
    tq = NA_ROWS * GRID_W
    return pl.pallas_call(
        _na_kernel,
        grid=(bn, GRID_H // NA_ROWS),
        in_specs=[pl.BlockSpec((1, tq, HEAD_GROUP), lambda b, i: (b, i, 0)),
                  pl.BlockSpec((1, SEQ, HEAD_GROUP), lambda b, i: (b, 0, 2)),
                  pl.BlockSpec((1, SEQ, HEAD_GROUP), lambda b, i: (b, 0, 4)),
                  pl.BlockSpec((1, CTX_LEN, HEAD_GROUP), lambda b, i: (b, 0, 0)),
                  pl.BlockSpec((1, CTX_LEN, HEAD_GROUP), lambda b, i: (b, 0, 2)),
                  pl.BlockSpec((1, NA_HEADS, NA_BAND // 2, tq, 2 * GRID_W),
                               lambda b, i: (_na_block_class(i), 0, 0, 0, 0))],
        out_specs=pl.BlockSpec((1, tq, HEAD_GROUP), lambda b, i: (b, i, 0)),
        out_shape=jax.ShapeDtypeStruct((bn, SEQ, HEAD_GROUP), BF16),
        compiler_params=_params("parallel", "arbitrary"),
        name="na_attention",
    )(z, z, z, zc, zc, bias)


def _diff_kernel(lq1_ref, lk1_ref, lq2_ref, lk2_ref, sub_ref, q_ref, k_ref, v_ref, kc_ref, vc_ref, o_ref):
    lam = (jnp.exp(jnp.sum(lq1_ref[...] * lk1_ref[...], axis=-1, keepdims=True))
           - jnp.exp(jnp.sum(lq2_ref[...] * lk2_ref[...], axis=-1, keepdims=True)) + LAM_INIT0)
    tq = q_ref.shape[1]
    lane = lax.broadcasted_iota(jnp.int32, (tq, LANES), 1)
    first = lane < DIFF_DIM

    def scores(h):
        cols = slice(h * LANES, (h + 1) * LANES)
        q2 = q_ref[0, :, cols]
        out = []
        for mp in range(2):
            qm = jnp.where(first if mp == 0 else jnp.logical_not(first), q2, jnp.zeros_like(q2))
            out.append((_dot_nt(qm, k_ref[0, :, cols]), _dot_nt(qm, kc_ref[0, :, cols])))
        return out

    nxt = scores(0)
    for h in range(DIFF_HEADS):
        cols = slice(h * LANES, (h + 1) * LANES)
        cur = nxt
        if h + 1 < DIFF_HEADS:
            nxt = scores(h + 1)
        maps = []
        for s_l, s_c in cur:
            m = jnp.maximum(jnp.max(s_l, axis=-1, keepdims=True), jnp.max(s_c, axis=-1, keepdims=True))
            p_l = jnp.exp2(s_l - m)
            p_c = jnp.exp2(s_c - m)
            maps.append((p_l, p_c, jnp.sum(p_l, axis=-1, keepdims=True) + jnp.sum(p_c, axis=-1, keepdims=True)))
        ratio = lam * maps[0][2] / maps[1][2]
        a_l = (maps[0][0] - ratio * maps[1][0]).astype(BF16)
        a_c = (maps[0][1] - ratio * maps[1][1]).astype(BF16)
        o = (_dot(a_l, v_ref[0, :, cols]) + _dot(a_c, vc_ref[0, :, cols])) / maps[0][2]
        ms = jnp.mean(o * o, axis=-1, keepdims=True)
        o = (o * lax.rsqrt(ms + EPS)) * sub_ref[...] * (1.0 - LAM_INIT0)
        o_ref[0, :, cols] = o.astype(BF16)


def _diff_attention(z, zc, lq1, lk1, lq2, lk2, subln):
    bn = z.shape[0]
    tq = DIFF_TQ
    vec = pl.BlockSpec((1, DIFF_DIM), lambda b, i: (0, 0))
    return pl.pallas_call(
        _diff_kernel,
        grid=(bn, SEQ // tq),
        in_specs=[vec, vec, vec, vec,
                  pl.BlockSpec((1, 2 * DIFF_DIM), lambda b, i: (0, 0)),
                  pl.BlockSpec((1, tq, HEAD_GROUP), lambda b, i: (b, i, 1)),
                  pl.BlockSpec((1, SEQ, HEAD_GROUP), lambda b, i: (b, 0, 3)),
                  pl.BlockSpec((1, SEQ, HEAD_GROUP), lambda b, i: (b, 0, 5)),
                  pl.BlockSpec((1, CTX_LEN, HEAD_GROUP), lambda b, i: (b, 0, 1)),
                  pl.BlockSpec((1, CTX_LEN, HEAD_GROUP), lambda b, i: (b, 0, 3))],
        out_specs=pl.BlockSpec((1, tq, HEAD_GROUP), lambda b, i: (b, i, 0)),
        out_shape=jax.ShapeDtypeStruct((bn, SEQ, HEAD_GROUP), BF16),
        compiler_params=_params("parallel", "arbitrary"),
        name="diff_attention",
    )(lq1.reshape(1, -1), lk1.reshape(1, -1), lq2.reshape(1, -1), lk2.reshape(1, -1), subln.reshape(1, -1),
      z, z, z, zc, zc)


def _out_kernel(x_ref, ya_ref, yb_ref, w_ref, g1_ref, n2_ref, sh2_ref, sc2_ref, wr_ref, x1_ref, h2_ref, aff_ref):
    half = ya_ref.shape[2]
    y = _dot(ya_ref[0], w_ref[:half, :]) + _dot(yb_ref[0], w_ref[half:, :])
    x1 = x_ref[0] + g1_ref[0] * y
    x1_ref[0] = x1
    hb = _norm_mod(x1, n2_ref[...], sh2_ref[0], sc2_ref[0]).astype(BF16)
    h2_ref[0] = hb
    logits = _dot_nt(wr_ref[...], hb)
    m = jnp.max(logits, axis=0, keepdims=True)
    p = jnp.exp(logits - m)
    aff_ref[0] = p / jnp.sum(p, axis=0, keepdims=True)


def _out_proj(x, ya, yb, w_out, gate1, norm2, shift2, scale2, w_router_t):
    bn, n, _ = x.shape
    tn = PROJ_TN
    half = ya.shape[2]
    tok = lambda b, i: (b, i, 0)
    per_b = lambda b, i: (b, 0, 0)
    return pl.pallas_call(
        _out_kernel,
        grid=(bn, n // tn),
        in_specs=[pl.BlockSpec((1, tn, D_MODEL), tok),
                  pl.BlockSpec((1, tn, half), tok),
                  pl.BlockSpec((1, tn, half), tok),
                  pl.BlockSpec((2 * half, D_MODEL), lambda b, i: (0, 0)),
                  pl.BlockSpec((1, 1, D_MODEL), per_b),
                  pl.BlockSpec((1, D_MODEL), lambda b, i: (0, 0)),
                  pl.BlockSpec((1, 1, D_MODEL), per_b),
                  pl.BlockSpec((1, 1, D_MODEL), per_b),
                  pl.BlockSpec((N_EXPERTS, D_MODEL), lambda b, i: (0, 0))],
        out_specs=[pl.BlockSpec((1, tn, D_MODEL), tok),
                   pl.BlockSpec((1, tn, D_MODEL), tok),
                   pl.BlockSpec((1, N_EXPERTS, tn), lambda b, i: (b, 0, i))],
        out_shape=[jax.ShapeDtypeStruct((bn, n, D_MODEL), F32),
                   jax.ShapeDtypeStruct((bn, n, D_MODEL), BF16),
                   jax.ShapeDtypeStruct((bn, N_EXPERTS, n), F32)],
        compiler_params=_params("parallel", "parallel"),
        name="out_proj",
    )(x, ya, yb, w_out, gate1, norm2.reshape(1, -1), shift2, scale2, w_router_t)


def _excl_cumsum_lanes(x01):
    e, n = x01.shape
    i = lax.broadcasted_iota(jnp.int32, (LANES, LANES), 0)
    j = lax.broadcasted_iota(jnp.int32, (LANES, LANES), 1)
    tri = jnp.where(i < j, 1.0, 0.0).astype(BF16)
    offset = jnp.zeros((e, 1), F32)
    chunks = []
    for c in range(n // LANES):
        xc = x01[:, c * LANES:(c + 1) * LANES]
        chunks.append(_dot(xc.astype(BF16), tri) + offset)
        offset = offset + jnp.sum(xc, axis=-1, keepdims=True)
    return jnp.concatenate(chunks, axis=-1), offset


def _route_kernel(aff_ref, pos_ref, g_ref, post_ref):
    nb, ne, n = aff_ref.shape
    a = aff_ref[...].reshape(nb * ne, n)
    thr = jnp.zeros((a.shape[0], 1), jnp.int32)
    for bit in range(30, -1, -1):
        cand = thr | (1 << bit)
        cnt = jnp.sum(jnp.where(a >= lax.bitcast_convert_type(cand, F32), 1.0, 0.0), axis=-1, keepdims=True)
        thr = jnp.where(cnt >= CAP, cand, thr)
    gt = jnp.where(a >= lax.bitcast_convert_type(thr + 1, F32), 1.0, 0.0)
    eq = jnp.where(a >= lax.bitcast_convert_type(thr, F32), 1.0, 0.0) - gt
    need = CAP - jnp.sum(gt, axis=-1, keepdims=True)
    eq_rank, _ = _excl_cumsum_lanes(eq)
    sel = gt + eq * jnp.where(eq_rank < need, 1.0, 0.0)
    pos, _ = _excl_cumsum_lanes(sel)
    pos = jnp.where(sel > 0.5, pos, -1.0)
    pos_ref[...] = pos.astype(jnp.int32).reshape(nb, ne, n)
    g_ref[...] = jnp.where(sel > 0.5, a, 0.0).reshape(nb, ne, n)
    fill = jnp.full((LANES - ne, n), -1.0, F32)
    for b in range(nb):
        padded = jnp.concatenate([pos[b * ne:(b + 1) * ne, :], fill], axis=0)
        post_ref[b] = padded.T.astype(jnp.int32)


def _route(aff_t):
    bn, e, n = aff_t.shape
    blk = pl.BlockSpec((ROUTE_B, e, n), lambda b: (b, 0, 0))
    return pl.pallas_call(
        _route_kernel,
        grid=(bn // ROUTE_B,),
        in_specs=[blk],
        out_specs=[blk, blk, pl.BlockSpec((ROUTE_B, n, LANES), lambda b: (b, 0, 0))],
        out_shape=[jax.ShapeDtypeStruct((bn, e, n), jnp.int32), jax.ShapeDtypeStruct((bn, e, n), F32),
                   jax.ShapeDtypeStruct((bn, n, LANES), jnp.int32)],
        compiler_params=_params("parallel"),
        name="route",
    )(aff_t)


def _one_hot_slots(pos_row):
    slot = lax.broadcasted_iota(jnp.int32, (CAP, pos_row.shape[1]), 0)
    return slot == pos_row


def _gather_kernel(pos_ref, g_ref, h_ref, x_ref, gs_ref):
    for e in range(N_EXPERTS):
        hit = _one_hot_slots(pos_ref[0, e:e + 1, :])
        x_ref[e] = _dot(jnp.where(hit, 1.0, 0.0).astype(BF16), h_ref[0]).astype(BF16)
        g = jnp.sum(jnp.where(hit, g_ref[0, e:e + 1, :], 0.0), axis=-1, keepdims=True)
        gs_ref[e] = jnp.broadcast_to(g, (CAP, LANES))


def _gather(pos, gsel, h2):
    bn, e, n = pos.shape
    rows = pl.BlockSpec((1, e, n), lambda b: (b, 0, 0))
    return pl.pallas_call(
        _gather_kernel,
        grid=(bn,),
        in_specs=[rows, rows, pl.BlockSpec((1, n, D_MODEL), lambda b: (b, 0, 0))],
        out_specs=[pl.BlockSpec((e, CAP, D_MODEL), lambda b: (0, b, 0)),
                   pl.BlockSpec((e, CAP, LANES), lambda b: (0, b, 0))],
        out_shape=[jax.ShapeDtypeStruct((e, bn * CAP, D_MODEL), BF16),
                   jax.ShapeDtypeStruct((e, bn * CAP, LANES), F32)],
        compiler_params=_params("parallel"),
        name="moe_gather",
    )(pos, gsel, h2)


def _ffn_kernel(x_ref, gs_ref, w1_ref, w3_ref, w2_ref, y_ref, acc_ref):
    f = pl.program_id(2)
    n_f = pl.num_programs(2)

    def step(first, last):
        w1 = w1_ref[0].astype(BF16)
        w3 = w3_ref[0].astype(BF16)
        w2 = w2_ref[0].astype(BF16)
        chunk = x_ref.shape[1] // FFN_ROW_CHUNKS
        for c in range(FFN_ROW_CHUNKS):
            rows = slice(c * chunk, (c + 1) * chunk)
            xs = x_ref[0, rows, :]
            a = _dot(xs, w1)
            hid = (a * _sigmoid(a)) * _dot(xs, w3)
            y = _dot(hid.astype(BF16), w2)
            if first:
                acc_ref[rows, :] = y
            elif last:
                g = gs_ref[0, rows, :]
                for j in range(D_MODEL // LANES):
                    cols = slice(j * LANES, (j + 1) * LANES)
                    y_ref[0, rows, cols] = ((acc_ref[rows, cols] + y[:, cols]) * g).astype(BF16)
            else:
                acc_ref[rows, :] += y

    pl.when(f == 0)(functools.partial(step, True, False))
    pl.when(jnp.logical_and(f > 0, f < n_f - 1))(functools.partial(step, False, False))
    pl.when(f == n_f - 1)(functools.partial(step, False, True))


def _expert_ffn(xg, gs, w1, w3, w2):
    e, rows, _ = xg.shape
    tm = rows // FFN_M_SPLIT
    n_f = D_EXPERT // FFN_TF
    return pl.pallas_call(
        _ffn_kernel,
        grid=(e, FFN_M_SPLIT, n_f),
        in_specs=[pl.BlockSpec((1, tm, D_MODEL), lambda ex, m, f: (ex, m, 0)),
                  pl.BlockSpec((1, tm, LANES), lambda ex, m, f: (ex, m, 0)),
                  pl.BlockSpec((1, D_MODEL, FFN_TF), lambda ex, m, f: (ex, 0, f)),
                  pl.BlockSpec((1, D_MODEL, FFN_TF), lambda ex, m, f: (ex, 0, f)),
                  pl.BlockSpec((1, FFN_TF, D_MODEL), lambda ex, m, f: (ex, f, 0))],
        out_specs=pl.BlockSpec((1, tm, D_MODEL), lambda ex, m, f: (ex, m, 0)),
        out_shape=jax.ShapeDtypeStruct((e, rows, D_MODEL), BF16),
        scratch_shapes=[pltpu.VMEM((tm, D_MODEL), F32)],
        compiler_params=_params("parallel", "parallel", "arbitrary"),
        name="expert_ffn",
    )(xg, gs, w1, w3, w2)


def _combine_kernel(post_ref, y_ref, x_ref, g2_ref, fin_ref, o_ref, *, final):
    tn = post_ref.shape[1]
    slot = lax.broadcasted_iota(jnp.int32, (tn, CAP), 1)
    hits = [jnp.where(post_ref[0, :, e:e + 1] == slot, 1.0, 0.0).astype(BF16) for e in range(N_EXPERTS)]
    moe = _dot(jnp.concatenate(hits, axis=-1), y_ref[...].reshape(N_EXPERTS * CAP, D_MODEL))
    x = x_ref[0] + g2_ref[0] * moe
    if final:
        ms = jnp.mean(x * x, axis=-1, keepdims=True)
        x = (x * lax.rsqrt(ms + EPS)) * fin_ref[...]
    o_ref[0] = x


def _combine(post, y, x, gate2, final_g, final):
    bn, n, _ = x.shape
    tn = COMBINE_TN
    tok = lambda b, i: (b, i, 0)
    return pl.pallas_call(
        functools.partial(_combine_kernel, final=final),
        grid=(bn, n // tn),
        in_specs=[pl.BlockSpec((1, tn, LANES), tok),
                  pl.BlockSpec((N_EXPERTS, CAP, D_MODEL), lambda b, i: (0, b, 0)),
                  pl.BlockSpec((1, tn, D_MODEL), tok),
                  pl.BlockSpec((1, 1, D_MODEL), lambda b, i: (b, 0, 0)),
                  pl.BlockSpec((1, D_MODEL), lambda b, i: (0, 0))],
        out_specs=pl.BlockSpec((1, tn, D_MODEL), tok),
        out_shape=jax.ShapeDtypeStruct((bn, n, D_MODEL), F32),
        compiler_params=_params("parallel", "arbitrary"),
        name="moe_combine",
    )(post, y, x, gate2, final_g.reshape(1, -1))


def _conv_proj_kernel(x_ref, g_ref, sh_ref, sc_ref, w_ref, v_ref, gb_ref, u_ref):
    h = _norm_mod(x_ref[0], g_ref[...], sh_ref[0], sc_ref[0]).astype(BF16)
    w = CONV_W
    xc = _dot(h, w_ref[:, 0 * w:1 * w])
    gate_c = _dot(h, w_ref[:, 2 * w:3 * w])
    v_ref[0] = gate_c * xc
    gb_ref[0] = _dot(h, w_ref[:, 1 * w:2 * w])
    glu_a = _dot(h, w_ref[:, 3 * w:4 * w])
    glu_g = _dot(h, w_ref[:, 4 * w:5 * w])
    u_ref[0] = glu_a * _sigmoid(glu_g)


def _conv_proj(x, g, shift, scale, w):
    bn, n, _ = x.shape
    tn = PROJ_TN
    tok = lambda b, i: (b, i, 0)
    per_b = lambda b, i: (b, 0, 0)
    narrow = pl.BlockSpec((1, tn, CONV_W), tok)
    vec = pl.BlockSpec((1, 1, D_MODEL), per_b)
    return pl.pallas_call(
        _conv_proj_kernel,
        grid=(bn, n // tn),
        in_specs=[pl.BlockSpec((1, tn, D_MODEL), tok),
                  pl.BlockSpec((1, D_MODEL), lambda b, i: (0, 0)),
                  vec, vec,
                  pl.BlockSpec((D_MODEL, CONV_IN), lambda b, i: (0, 0))],
        out_specs=[narrow, narrow, narrow],
        out_shape=[jax.ShapeDtypeStruct((bn, n, CONV_W), F32)] * 3,
        compiler_params=_params("parallel", "parallel"),
        name="conv_proj",
    )(x, g.reshape(1, -1), shift, scale, w)


def _depthwise(win, w_ref, cols, taps, rows):
    offs = [CONV_HALO + k - taps // 2 for k in range(taps)]
    acc = None
    for b in range(8):
        ks = [k for k in range(taps) if offs[k] % 8 == b]
        if not ks:
            continue
        shifted = pltpu.roll(win, win.shape[0] - b, 0) if b else win
        for k in ks:
            term = shifted[offs[k] - b:offs[k] - b + rows, :] * w_ref[k:k + 1, cols]
            acc = term if acc is None else acc + term
    return acc


def _conv_kernel(v_ref, gb_ref, u_ref, scw_ref, cfw_ref, cfb_ref, lng_ref, lnb_ref, yc_ref, yd_ref, vpad_ref, upad_ref):
    i = pl.program_id(1)
    n = v_ref.shape[1]
    tn = gb_ref.shape[1]

    @pl.when(i == 0)
    def _stage():
        zeros = jnp.zeros((CONV_HALO, CONV_W), F32)
        for pad_ref, src_ref in ((vpad_ref, v_ref), (upad_ref, u_ref)):
            pad_ref[0:CONV_HALO, :] = zeros
            pad_ref[CONV_HALO:CONV_HALO + n, :] = src_ref[0]
            pad_ref[CONV_HALO + n:, :] = zeros

    for j in range(tn // CONV_SUB):
        rows = slice(j * CONV_SUB, (j + 1) * CONV_SUB)
        start = pl.multiple_of(i * tn + j * CONV_SUB, CONV_SUB)
        yd_parts = []
        for s in range(CONV_W // LANES):
            cols = slice(s * LANES, (s + 1) * LANES)
            vwin = vpad_ref[pl.ds(start, CONV_SUB + 2 * CONV_HALO), cols]
            yc = gb_ref[0, rows, cols] * _depthwise(vwin, scw_ref, cols, SC_K, CONV_SUB)
            yc_ref[0, rows, cols] = yc.astype(BF16)
            uwin = upad_ref[pl.ds(start, CONV_SUB + 2 * CONV_HALO), cols]
            yd_parts.append(_depthwise(uwin, cfw_ref, cols, CF_K, CONV_SUB) + cfb_ref[:, cols])
        u = jnp.concatenate(yd_parts, axis=-1)
        mu = jnp.mean(u, axis=-1, keepdims=True)
        d = u - mu
        var = jnp.mean(d * d, axis=-1, keepdims=True)
        ln = (d * lax.rsqrt(var + EPS)) * lng_ref[...] + lnb_ref[...]
        yd_ref[0, rows, :] = (ln * _sigmoid(ln)).astype(BF16)


def _convs(v, gb, u, sc_w, cf_w, cf_b, ln_g, ln_b):
    bn, n, w = v.shape
    tn = CONV_TN
    full = pl.BlockSpec((1, n, w), lambda b, i: (b, 0, 0))
    tile = pl.BlockSpec((1, tn, w), lambda b, i: (b, i, 0))
    vec = pl.BlockSpec((1, w), lambda b, i: (0, 0))
    return pl.pallas_call(
        _conv_kernel,
        grid=(bn, n // tn),
        in_specs=[full, tile, full,
                  pl.BlockSpec((SC_K, w), lambda b, i: (0, 0)),
                  pl.BlockSpec((CF_K, w), lambda b, i: (0, 0)),
                  vec, vec, vec],
        out_specs=[tile, tile],
        out_shape=[jax.ShapeDtypeStruct((bn, n, w), BF16)] * 2,
        scratch_shapes=[pltpu.VMEM((n + 2 * CONV_HALO, w), F32)] * 2,
        compiler_params=_params("parallel", "arbitrary"),
        name="convs",
    )(v, gb, u, sc_w, cf_w, cf_b.reshape(1, -1), ln_g.reshape(1, -1), ln_b.reshape(1, -1))


def _split_mod(mod, bn):
    return [mod[:bn, j * D_MODEL:(j + 1) * D_MODEL].reshape(bn, 1, D_MODEL) for j in range(6)]


def _moe(x, gate2, aff_t, h2, w1, w3, w2, final_g, final):
    pos, gsel, post = _route(aff_t)
    xg, gs = _gather(pos, gsel, h2)
    y = _expert_ffn(xg, gs, w1, w3, w2)
    return _combine(post, y, x, gate2, final_g, final)


def kernel(x, c, ctx, c_ctx, l0_norm1, l0_w_mod, l0_b_mod, l0_w_in, l0_w_out, l0_rpb, l0_lam_q1, l0_lam_k1, l0_lam_q2, l0_lam_k2, l0_subln, l0_norm2, l0_w_router, l0_w1, l0_w3, l0_w2, l1_norm1, l1_w_mod, l1_b_mod, l1_w_in, l1_w_out, l1_sc_w, l1_cf_w, l1_cf_b, l1_ln_g, l1_ln_b, l1_norm2, l1_w_router, l1_w1, l1_w3, l1_w2, final_norm):
    bn = x.shape[0]
    pad_rows = -(bn + 1) % 8
    cc = jnp.concatenate([c, c_ctx[None, :], jnp.zeros((pad_rows, D_MODEL), F32)], axis=0)

    mod0 = _modulation(cc, l0_w_mod, l0_b_mod)
    shift1, scale1, gate1, shift2, scale2, gate2 = _split_mod(mod0, bn)
    ctx_shift = jnp.broadcast_to(mod0[bn, :D_MODEL].reshape(1, 1, D_MODEL), (bn, 1, D_MODEL))
    ctx_scale = jnp.broadcast_to(mod0[bn, D_MODEL:2 * D_MODEL].reshape(1, 1, D_MODEL), (bn, 1, D_MODEL))
    w_in0 = l0_w_in.astype(BF16)
    cos_t, sin_t = _rope_tables()
    z = _attn_proj(x, l0_norm1, shift1, scale1, w_in0, cos_t, sin_t, rope_groups=(1, 3), scaled_groups=(0, 1))
    zc = _attn_proj(ctx, l0_norm1, ctx_shift, ctx_scale, w_in0[:, 2 * HEAD_GROUP:], cos_t[:CTX_LEN], sin_t[:CTX_LEN],
                    rope_groups=(), scaled_groups=())
    y_a = _na_attention(z, zc, _na_bias_table(l0_rpb))
    y_b = _diff_attention(z, zc, l0_lam_q1, l0_lam_k1, l0_lam_q2, l0_lam_k2, l0_subln)
    x1, h2, aff_t = _out_proj(x, y_a, y_b, l0_w_out.astype(BF16), gate1, l0_norm2, shift2, scale2,
                              l0_w_router.T.astype(BF16))
    x2 = _moe(x1, gate2, aff_t, h2, l0_w1, l0_w3, l0_w2, final_norm, final=False)

    mod1 = _modulation(cc, l1_w_mod, l1_b_mod)
    shift1b, scale1b, gate1b, shift2b, scale2b, gate2b = _split_mod(mod1, bn)
    v, gb, u = _conv_proj(x2, l1_norm1, shift1b, scale1b, l1_w_in.astype(BF16))
    y_c, y_d = _convs(v, gb, u, l1_sc_w, l1_cf_w, l1_cf_b, l1_ln_g, l1_ln_b)
    x3, h2b, aff_tb = _out_proj(x2, y_c, y_d, l1_w_out.astype(BF16), gate1b, l1_norm2, shift2b, scale2b,
                                l1_w_router.T.astype(BF16))
    return _moe(x3, gate2b, aff_tb, h2b, l1_w1, l1_w3, l1_w2, final_norm, final=True)
```

```python
import functools
import math

import numpy as np
import jax
import jax.numpy as jnp
from jax import lax
from jax.experimental import pallas as pl
from jax.experimental.pallas import tpu as pltpu

D_MODEL = 1024
BATCH = 16
SEQ = 2048
GRID_W = 64
GRID_H = SEQ // GRID_W
CTX_LEN = 256
NA_HEADS = 8
NA_WIN_H = 8
NA_WIN_W = 16
DIFF_HEADS = 4
DIFF_DIM = 64
HEAD_GROUP = 512
ATTN_IN = 6 * HEAD_GROUP
ROPE_THETA = 10000.0
SC_K = 3
CF_K = 31
CONV_W = 512
CONV_IN = 5 * CONV_W
N_EXPERTS = 16
D_EXPERT = 2816
CAP = 2 * SEQ // N_EXPERTS
EPS = 1e-6
LAM_INIT0 = 0.8 - 0.6 * math.exp(-0.3 * 0)
NEG = -1e30
LOG2E = math.log2(math.e)
Q_SCALE = DIFF_DIM ** -0.5 * LOG2E

LANES = 128
NA_ROWS = 4
NA_BAND = 12
PROJ_TN = 1024
DIFF_TQ = 512
CONV_TN = 512
CONV_HALO = 16
CONV_SUB = 128
ROUTE_B = 8
FFN_TF = 256
FFN_M_SPLIT = 2
FFN_ROW_CHUNKS = 2
COMBINE_TN = 1024
VMEM_LIMIT = 60 * 1024 * 1024

F32 = jnp.float32
BF16 = jnp.bfloat16


def _dot(a, b):
    return jnp.dot(a, b, preferred_element_type=F32)


def _dot_nt(a, b):
    return lax.dot_general(a, b, (((1,), (1,)), ((), ())), preferred_element_type=F32)


def _dot_tn(a, b):
    return lax.dot_general(a, b, (((0,), (0,)), ((), ())), preferred_element_type=F32)


def _sigmoid(x):
    return 1.0 / (1.0 + jnp.exp(-x))


def _params(*sem):
    return pltpu.CompilerParams(dimension_semantics=sem, vmem_limit_bytes=VMEM_LIMIT)


def _norm_mod(x, g, shift, scale):
    ms = jnp.mean(x * x, axis=-1, keepdims=True)
    return (x * lax.rsqrt(ms + EPS)) * g * (1.0 + scale) + shift


def _mod_kernel(c_ref, w_ref, b_ref, o_ref):
    cc = c_ref[...]
    s = cc * _sigmoid(cc)
    o_ref[...] = _dot(s.astype(BF16), w_ref[...].astype(BF16)) + b_ref[...]


def _modulation(cc, w_mod, b_mod):
    rows = cc.shape[0]
    tc = D_MODEL
    return pl.pallas_call(
        _mod_kernel,
        grid=(w_mod.shape[1] // tc,),
        in_specs=[pl.BlockSpec((rows, D_MODEL), lambda j: (0, 0)),
                  pl.BlockSpec((D_MODEL, tc), lambda j: (0, j)),
                  pl.BlockSpec((1, tc), lambda j: (0, j))],
        out_specs=pl.BlockSpec((rows, tc), lambda j: (0, j)),
        out_shape=jax.ShapeDtypeStruct((rows, w_mod.shape[1]), F32),
        compiler_params=_params("arbitrary"),
        name="modulation",
    )(cc, w_mod, b_mod.reshape(1, -1))


def _rope(z, cos, sin):
    lane = lax.broadcasted_iota(jnp.int32, z.shape, 1)
    partner = jnp.where((lane & 31) < 16, pltpu.roll(z, LANES - 16, 1), pltpu.roll(z, 16, 1))
    return z * cos + partner * sin


def _attn_proj_kernel(x_ref, g_ref, sh_ref, sc_ref, w_ref, cos_ref, sin_ref, o_ref, *, rope_groups, scaled_groups):
    h = _norm_mod(x_ref[0], g_ref[...], sh_ref[0], sc_ref[0]).astype(BF16)
    n_groups = w_ref.shape[1] // HEAD_GROUP
    for j in range(n_groups):
        z = _dot(h, w_ref[:, j * HEAD_GROUP:(j + 1) * HEAD_GROUP])
        for s in range(HEAD_GROUP // LANES):
            zz = z[:, s * LANES:(s + 1) * LANES]
            if j in rope_groups:
                zz = _rope(zz, cos_ref[...], sin_ref[...])
            if j in scaled_groups:
                zz = zz * Q_SCALE
            o_ref[0, :, j * HEAD_GROUP + s * LANES:j * HEAD_GROUP + (s + 1) * LANES] = zz.astype(BF16)


def _attn_proj(x, g, shift, scale, w, cos_t, sin_t, rope_groups, scaled_groups):
    bn, n, _ = x.shape
    tn = min(PROJ_TN, n)
    cols = w.shape[1]
    return pl.pallas_call(
        functools.partial(_attn_proj_kernel, rope_groups=rope_groups, scaled_groups=scaled_groups),
        grid=(bn, n // tn),
        in_specs=[pl.BlockSpec((1, tn, D_MODEL), lambda b, i: (b, i, 0)),
                  pl.BlockSpec((1, D_MODEL), lambda b, i: (0, 0)),
                  pl.BlockSpec((1, 1, D_MODEL), lambda b, i: (b, 0, 0)),
                  pl.BlockSpec((1, 1, D_MODEL), lambda b, i: (b, 0, 0)),
                  pl.BlockSpec((D_MODEL, cols), lambda b, i: (0, 0)),
                  pl.BlockSpec((tn, LANES), lambda b, i: (i, 0)),
                  pl.BlockSpec((tn, LANES), lambda b, i: (i, 0))],
        out_specs=pl.BlockSpec((1, tn, cols), lambda b, i: (b, i, 0)),
        out_shape=jax.ShapeDtypeStruct((bn, n, cols), BF16),
        compiler_params=_params("parallel", "parallel"),
        name="attn_proj",
    )(x, g.reshape(1, -1), shift, scale, w, cos_t, sin_t)


def _rope_tables():
    n_freq = DIFF_DIM // 4
    freqs = ROPE_THETA ** (-jnp.arange(n_freq, dtype=F32) / n_freq)
    t = jnp.arange(SEQ)
    row, col = t // GRID_W, t % GRID_W

    def half(pos):
        ang = pos.astype(F32)[:, None] * freqs
        c, s = jnp.cos(ang), jnp.sin(ang)
        return jnp.concatenate([c, c], axis=-1), jnp.concatenate([-s, s], axis=-1)

    cr, sr = half(row)
    cc, sc = half(col)
    cos64 = jnp.concatenate([cr, cc], axis=-1)
    sin64 = jnp.concatenate([sr, sc], axis=-1)
    return jnp.tile(cos64, (1, LANES // DIFF_DIM)), jnp.tile(sin64, (1, LANES // DIFF_DIM))


def _na_bias_table(rpb):
    n_dr, n_dc = 2 * NA_WIN_H - 1, 2 * NA_WIN_W - 1
    c = np.arange(GRID_W)
    cs = np.clip(c - NA_WIN_W // 2, 0, GRID_W - NA_WIN_W)
    kc = np.arange(GRID_W)
    col_ok = (kc[None, :] >= cs[:, None]) & (kc[None, :] < cs[:, None] + NA_WIN_W)
    dc = kc[None, :] - c[:, None] + NA_WIN_W - 1
    one_hot = (dc[None] == np.arange(n_dc)[:, None, None]) & col_ok[None]
    pair_hot = np.zeros((2, n_dc, GRID_W, 2, GRID_W), np.float32)
    for k2 in range(2):
        pair_hot[k2, :, :, k2, :] = one_hot
    pair_hot = pair_hot.reshape(2 * n_dc, GRID_W, 2 * GRID_W)
    padded = jnp.pad(rpb.astype(F32) * LOG2E, ((0, 0), (1, 1), (0, 0)))
    pairs = jnp.concatenate([padded[:, :-1], padded[:, 1:]], axis=-1)
    slabs = jnp.einsum('hsd,dcn->hscn', pairs, jnp.asarray(pair_hot), precision=lax.Precision.HIGHEST)
    blocks = np.array([0, 2 * NA_ROWS, GRID_H - NA_ROWS])
    starts = np.clip(blocks - NA_WIN_H // 2, 0, GRID_H - NA_BAND)
    r = blocks[:, None] + np.arange(NA_ROWS)[None, :]
    rs = np.clip(r - NA_WIN_H // 2, 0, GRID_H - NA_WIN_H)
    krow = starts[:, None, None] + np.arange(NA_BAND)[None, None, :]
    row_ok = (krow >= rs[:, :, None]) & (krow < rs[:, :, None] + NA_WIN_H)
    dr_even = krow[:, :, 0::2] - r[:, :, None] + NA_WIN_H - 1
    slab_idx = np.clip(dr_even + 1, 0, n_dr)
    picked = jnp.take(slabs, jnp.asarray(slab_idx.reshape(-1)), axis=1)
    picked = picked.reshape(NA_HEADS, len(blocks), NA_ROWS, NA_BAND // 2, GRID_W, 2 * GRID_W)
    row_ok_lanes = np.repeat(row_ok.reshape(len(blocks), NA_ROWS, NA_BAND // 2, 2), GRID_W, axis=-1)
    ok = row_ok_lanes[None, :, :, :, None, :] & np.tile(col_ok, (1, 2))[None, None, None, None, :, :]
    table = jnp.where(jnp.asarray(ok), picked, NEG).transpose(1, 0, 3, 2, 4, 5)
    return table.reshape(len(blocks), NA_HEADS, NA_BAND // 2, NA_ROWS * GRID_W, 2 * GRID_W)


def _na_band_start(i):
    return jnp.clip(i * NA_ROWS - NA_WIN_H // 2, 0, GRID_H - NA_BAND)


def _na_kernel(q_ref, k_ref, v_ref, kc_ref, vc_ref, bias_ref, o_ref):
    start = pl.multiple_of(_na_band_start(pl.program_id(1)) * GRID_W, GRID_W)
    band = NA_BAND * GRID_W
    lane = lax.broadcasted_iota(jnp.int32, (q_ref.shape[1], LANES), 1)
    first = lane < (LANES // 2)
    for p in range(NA_HEADS // 2):
        cols = slice(p * LANES, (p + 1) * LANES)
        q2 = q_ref[0, :, cols]
        kb = k_ref[0, pl.ds(start, band), cols]
        kc = kc_ref[0, :, cols]
        vb = jnp.concatenate([v_ref[0, pl.ds(start, band), cols], jnp.ones((band, LANES), BF16)], axis=-1)
        vc = jnp.concatenate([vc_ref[0, :, cols], jnp.ones((kc.shape[0], LANES), BF16)], axis=-1)
        outs = []
        for par in range(2):
            qm = jnp.where(first if par == 0 else jnp.logical_not(first), q2, jnp.zeros_like(q2))
            s_w = _dot_nt(qm, kb)
            s_w = jnp.concatenate([s_w[:, m * LANES:(m + 1) * LANES] + bias_ref[0, 2 * p + par, m]
                                   for m in range(NA_BAND // 2)], axis=-1)
            s_c = _dot_nt(qm, kc)
            m = jnp.maximum(jnp.max(s_w, axis=-1, keepdims=True), jnp.max(s_c, axis=-1, keepdims=True))
            p_w = jnp.exp2(s_w - m)
            p_c = jnp.exp2(s_c - m)
            o = _dot(p_w.astype(BF16), vb) + _dot(p_c.astype(BF16), vc)
            outs.append(o[:, :LANES] / o[:, LANES:])
        o_ref[0, :, cols] = jnp.where(first, outs[0], outs[1]).astype(BF16)


def _na_block_class(i):
    return (i * NA_ROWS - _na_band_start(i)) // NA_ROWS


def _na_attention(z, zc, bias):
    bn = z.shape[0]
    tq = NA_ROWS * GRID_W
    return pl.pallas_call(
        _na_kernel,
        grid=(bn, GRID_H // NA_ROWS),
        in_specs=[pl.BlockSpec((1, tq, HEAD_GROUP), lambda b, i: (b, i, 0)),
                  pl.BlockSpec((1, SEQ, HEAD_GROUP), lambda b, i: (b, 0, 2)),
                  pl.BlockSpec((1, SEQ, HEAD_GROUP), lambda b, i: (b, 0, 4)),
                  pl.BlockSpec((1, CTX_LEN, HEAD_GROUP), lambda b, i: (b, 0, 0)),
                  pl.BlockSpec((1, CTX_LEN, HEAD_GROUP), lambda b, i: (b, 0, 2)),
                  pl.BlockSpec((1, NA_HEADS, NA_BAND // 2, tq, 2 * GRID_W),
                               lambda b, i: (_na_block_class(i), 0, 0, 0, 0))],
        out_specs=pl.BlockSpec((1, tq, HEAD_GROUP), lambda b, i: (b, i, 0)),
        out_shape=jax.ShapeDtypeStruct((bn, SEQ, HEAD_GROUP), BF16),
        compiler_params=_params("parallel", "arbitrary"),
        name="na_attention",
    )(z, z, z, zc, zc, bias)


def _diff_kernel(lq1_ref, lk1_ref, lq2_ref, lk2_ref, sub_ref, q_ref, k_ref, v_ref, kc_ref, vc_ref, o_ref):
    lam = (jnp.exp(jnp.sum(lq1_ref[...] * lk1_ref[...], axis=-1, keepdims=True))
           - jnp.exp(jnp.sum(lq2_ref[...] * lk2_ref[...], axis=-1, keepdims=True)) + LAM_INIT0)
    tq = q_ref.shape[1]
    lane = lax.broadcasted_iota(jnp.int32, (tq, LANES), 1)
    first = lane < DIFF_DIM

    def scores(h):
        cols = slice(h * LANES, (h + 1) * LANES)
        q2 = q_ref[0, :, cols]
        out = []
        for mp in range(2):
            qm = jnp.where(first if mp == 0 else jnp.logical_not(first), q2, jnp.zeros_like(q2))
            out.append((_dot_nt(qm, k_ref[0, :, cols]), _dot_nt(qm, kc_ref[0, :, cols])))
        return out

    nxt = scores(0)
    for h in range(DIFF_HEADS):
        cols = slice(h * LANES, (h + 1) * LANES)
        cur = nxt
        if h + 1 < DIFF_HEADS:
            nxt = scores(h + 1)
        v_l = jnp.concatenate([v_ref[0, :, cols], jnp.ones((v_ref.shape[1], LANES), BF16)], axis=-1)
        v_c = jnp.concatenate([vc_ref[0, :, cols], jnp.ones((vc_ref.shape[1], LANES), BF16)], axis=-1)
        maps = []
        for s_l, s_c in cur:
            m = jnp.maximum(jnp.max(s_l, axis=-1, keepdims=True), jnp.max(s_c, axis=-1, keepdims=True))
            pv = _dot(jnp.exp2(s_l - m).astype(BF16), v_l) + _dot(jnp.exp2(s_c - m).astype(BF16), v_c)
            maps.append(pv[:, :LANES] / pv[:, LANES:])
        o = maps[0] - lam * maps[1]
        ms = jnp.mean(o * o, axis=-1, keepdims=True)
        o = (o * lax.rsqrt(ms + EPS)) * sub_ref[...] * (1.0 - LAM_INIT0)
        o_ref[0, :, cols] = o.astype(BF16)


def _diff_attention(z, zc, lq1, lk1, lq2, lk2, subln):
    bn = z.shape[0]
    tq = DIFF_TQ
    vec = pl.BlockSpec((1, DIFF_DIM), lambda b, i: (0, 0))
    return pl.pallas_call(
        _diff_kernel,
        grid=(bn, SEQ // tq),
        in_specs=[vec, vec, vec, vec,
                  pl.BlockSpec((1, 2 * DIFF_DIM), lambda b, i: (0, 0)),
                  pl.BlockSpec((1, tq, HEAD_GROUP), lambda b, i: (b, i, 1)),
                  pl.BlockSpec((1, SEQ, HEAD_GROUP), lambda b, i: (b, 0, 3)),
                  pl.BlockSpec((1, SEQ, HEAD_GROUP), lambda b, i: (b, 0, 5)),
                  pl.BlockSpec((1, CTX_LEN, HEAD_GROUP), lambda b, i: (b, 0, 1)),
                  pl.BlockSpec((1, CTX_LEN, HEAD_GROUP), lambda b, i: (b, 0, 3))],
        out_specs=pl.BlockSpec((1, tq, HEAD_GROUP), lambda b, i: (b, i, 0)),
        out_shape=jax.ShapeDtypeStruct((bn, SEQ, HEAD_GROUP), BF16),
        compiler_params=_params("parallel", "arbitrary"),
        name="diff_attention",
    )(lq1.reshape(1, -1), lk1.reshape(1, -1), lq2.reshape(1, -1), lk2.reshape(1, -1), subln.reshape(1, -1),
      z, z, z, zc, zc)


def _out_kernel(x_ref, ya_ref, yb_ref, w_ref, g1_ref, n2_ref, sh2_ref, sc2_ref, wr_ref, x1_ref, h2_ref, aff_ref):
    half = ya_ref.shape[2]
    y = _dot(ya_ref[0], w_ref[:half, :]) + _dot(yb_ref[0], w_ref[half:, :])
    x1 = x_ref[0] + g1_ref[0] * y
    x1_ref[0] = x1
    hb = _norm_mod(x1, n2_ref[...], sh2_ref[0], sc2_ref[0]).astype(BF16)
    h2_ref[0] = hb
    logits = _dot_nt(wr_ref[...], hb)
    m = jnp.max(logits, axis=0, keepdims=True)
    p = jnp.exp(logits - m)
    aff_ref[0] = p / jnp.sum(p, axis=0, keepdims=True)


def _out_proj(x, ya, yb, w_out, gate1, norm2, shift2, scale2, w_router_t):
    bn, n, _ = x.shape
    tn = PROJ_TN
    half = ya.shape[2]
    tok = lambda b, i: (b, i, 0)
    per_b = lambda b, i: (b, 0, 0)
    return pl.pallas_call(
        _out_kernel,
        grid=(bn, n // tn),
        in_specs=[pl.BlockSpec((1, tn, D_MODEL), tok),
                  pl.BlockSpec((1, tn, half), tok),
                  pl.BlockSpec((1, tn, half), tok),
                  pl.BlockSpec((2 * half, D_MODEL), lambda b, i: (0, 0)),
                  pl.BlockSpec((1, 1, D_MODEL), per_b),
                  pl.BlockSpec((1, D_MODEL), lambda b, i: (0, 0)),
                  pl.BlockSpec((1, 1, D_MODEL), per_b),
                  pl.BlockSpec((1, 1, D_MODEL), per_b),
                  pl.BlockSpec((N_EXPERTS, D_MODEL), lambda b, i: (0, 0))],
        out_specs=[pl.BlockSpec((1, tn, D_MODEL), tok),
                   pl.BlockSpec((1, tn, D_MODEL), tok),
                   pl.BlockSpec((1, N_EXPERTS, tn), lambda b, i: (b, 0, i))],
        out_shape=[jax.ShapeDtypeStruct((bn, n, D_MODEL), F32),
                   jax.ShapeDtypeStruct((bn, n, D_MODEL), BF16),
                   jax.ShapeDtypeStruct((bn, N_EXPERTS, n), F32)],
        compiler_params=_params("parallel", "parallel"),
        name="out_proj",
    )(x, ya, yb, w_out, gate1, norm2.reshape(1, -1), shift2, scale2, w_router_t)


def _excl_cumsum_lanes(x01):
    e, n = x01.shape
    i = lax.broadcasted_iota(jnp.int32, (LANES, LANES), 0)
    j = lax.broadcasted_iota(jnp.int32, (LANES, LANES), 1)
    tri = jnp.where(i < j, 1.0, 0.0).astype(BF16)
    offset = jnp.zeros((e, 1), F32)
    chunks = []
    for c in range(n // LANES):
        xc = x01[:, c * LANES:(c + 1) * LANES]
        chunks.append(_dot(xc.astype(BF16), tri) + offset)
        offset = offset + jnp.sum(xc, axis=-1, keepdims=True)
    return jnp.concatenate(chunks, axis=-1), offset


def _route_kernel(aff_ref, pos_ref, g_ref, post_ref):
    nb, ne, n = aff_ref.shape
    a = aff_ref[...].reshape(nb * ne, n)
    thr = jnp.zeros((a.shape[0], 1), jnp.int32)
    for bit in range(30, -1, -1):
        cand = thr | (1 << bit)
        cnt = jnp.sum(jnp.where(a >= lax.bitcast_convert_type(cand, F32), 1.0, 0.0), axis=-1, keepdims=True)
        thr = jnp.where(cnt >= CAP, cand, thr)
    gt = jnp.where(a >= lax.bitcast_convert_type(thr + 1, F32), 1.0, 0.0)
    eq = jnp.where(a >= lax.bitcast_convert_type(thr, F32), 1.0, 0.0) - gt
    need = CAP - jnp.sum(gt, axis=-1, keepdims=True)
    eq_rank, _ = _excl_cumsum_lanes(eq)
    sel = gt + eq * jnp.where(eq_rank < need, 1.0, 0.0)
    pos, _ = _excl_cumsum_lanes(sel)
    pos = jnp.where(sel > 0.5, pos, -1.0)
    pos_ref[...] = pos.astype(jnp.int32).reshape(nb, ne, n)
    g_ref[...] = jnp.where(sel > 0.5, a, 0.0).reshape(nb, ne, n)
    fill = jnp.full((LANES - ne, n), -1.0, F32)
    for b in range(nb):
        padded = jnp.concatenate([pos[b * ne:(b + 1) * ne, :], fill], axis=0)
        post_ref[b] = padded.T.astype(jnp.int32)


def _route(aff_t):
    bn, e, n = aff_t.shape
    blk = pl.BlockSpec((ROUTE_B, e, n), lambda b: (b, 0, 0))
    return pl.pallas_call(
        _route_kernel,
        grid=(bn // ROUTE_B,),
        in_specs=[blk],
        out_specs=[blk, blk, pl.BlockSpec((ROUTE_B, n, LANES), lambda b: (b, 0, 0))],
        out_shape=[jax.ShapeDtypeStruct((bn, e, n), jnp.int32), jax.ShapeDtypeStruct((bn, e, n), F32),
                   jax.ShapeDtypeStruct((bn, n, LANES), jnp.int32)],
        compiler_params=_params("parallel"),
        name="route",
    )(aff_t)


def _one_hot_slots(pos_row):
    slot = lax.broadcasted_iota(jnp.int32, (CAP, pos_row.shape[1]), 0)
    return slot == pos_row


def _gather_kernel(pos_ref, g_ref, h_ref, x_ref, gs_ref):
    for e in range(N_EXPERTS):
        hit = _one_hot_slots(pos_ref[0, e:e + 1, :])
        x_ref[e] = _dot(jnp.where(hit, 1.0, 0.0).astype(BF16), h_ref[0]).astype(BF16)
        g = jnp.sum(jnp.where(hit, g_ref[0, e:e + 1, :], 0.0), axis=-1, keepdims=True)
        gs_ref[e] = jnp.broadcast_to(g, (CAP, LANES))


def _gather(pos, gsel, h2):
    bn, e, n = pos.shape
    rows = pl.BlockSpec((1, e, n), lambda b: (b, 0, 0))
    return pl.pallas_call(
        _gather_kernel,
        grid=(bn,),
        in_specs=[rows, rows, pl.BlockSpec((1, n, D_MODEL), lambda b: (b, 0, 0))],
        out_specs=[pl.BlockSpec((e, CAP, D_MODEL), lambda b: (0, b, 0)),
                   pl.BlockSpec((e, CAP, LANES), lambda b: (0, b, 0))],
        out_shape=[jax.ShapeDtypeStruct((e, bn * CAP, D_MODEL), BF16),
                   jax.ShapeDtypeStruct((e, bn * CAP, LANES), F32)],
        compiler_params=_params("parallel"),
        name="moe_gather",
    )(pos, gsel, h2)


def _ffn_kernel(x_ref, gs_ref, w1_ref, w3_ref, w2_ref, y_ref, acc_ref):
    f = pl.program_id(2)
    n_f = pl.num_programs(2)

    def step(first, last):
        w1 = w1_ref[0].astype(BF16)
        w3 = w3_ref[0].astype(BF16)
        w2 = w2_ref[0].astype(BF16)
        chunk = x_ref.shape[1] // FFN_ROW_CHUNKS
        for c in range(FFN_ROW_CHUNKS):
            rows = slice(c * chunk, (c + 1) * chunk)
            xs = x_ref[0, rows, :]
            a = _dot(xs, w1)
            hid = (a * _sigmoid(a)) * _dot(xs, w3)
            y = _dot(hid.astype(BF16), w2)
            if first:
                acc_ref[rows, :] = y
            elif last:
                g = gs_ref[0, rows, :]
                for j in range(D_MODEL // LANES):
                    cols = slice(j * LANES, (j + 1) * LANES)
                    y_ref[0, rows, cols] = ((acc_ref[rows, cols] + y[:, cols]) * g).astype(BF16)
            else:
                acc_ref[rows, :] += y

    pl.when(f == 0)(functools.partial(step, True, False))
    pl.when(jnp.logical_and(f > 0, f < n_f - 1))(functools.partial(step, False, False))
    pl.when(f == n_f - 1)(functools.partial(step, False, True))


def _expert_ffn(xg, gs, w1, w3, w2):
    e, rows, _ = xg.shape
    tm = rows // FFN_M_SPLIT
    n_f = D_EXPERT // FFN_TF
    return pl.pallas_call(
        _ffn_kernel,
        grid=(e, FFN_M_SPLIT, n_f),
        in_specs=[pl.BlockSpec((1, tm, D_MODEL), lambda ex, m, f: (ex, m, 0)),
                  pl.BlockSpec((1, tm, LANES), lambda ex, m, f: (ex, m, 0)),
                  pl.BlockSpec((1, D_MODEL, FFN_TF), lambda ex, m, f: (ex, 0, f)),
                  pl.BlockSpec((1, D_MODEL, FFN_TF), lambda ex, m, f: (ex, 0, f)),
                  pl.BlockSpec((1, FFN_TF, D_MODEL), lambda ex, m, f: (ex, f, 0))],
        out_specs=pl.BlockSpec((1, tm, D_MODEL), lambda ex, m, f: (ex, m, 0)),
        out_shape=jax.ShapeDtypeStruct((e, rows, D_MODEL), BF16),
        scratch_shapes=[pltpu.VMEM((tm, D_MODEL), F32)],
        compiler_params=_params("parallel", "parallel", "arbitrary"),
        name="expert_ffn",
    )(xg, gs, w1, w3, w2)


def _combine_kernel(post_ref, y_ref, x_ref, g2_ref, fin_ref, o_ref, *, final):
    tn = post_ref.shape[1]
    slot = lax.broadcasted_iota(jnp.int32, (tn, CAP), 1)
    hits = [jnp.where(post_ref[0, :, e:e + 1] == slot, 1.0, 0.0).astype(BF16) for e in range(N_EXPERTS)]
    moe = _dot(jnp.concatenate(hits, axis=-1), y_ref[...].reshape(N_EXPERTS * CAP, D_MODEL))
    x = x_ref[0] + g2_ref[0] * moe
    if final:
        ms = jnp.mean(x * x, axis=-1, keepdims=True)
        x = (x * lax.rsqrt(ms + EPS)) * fin_ref[...]
    o_ref[0] = x


def _combine(post, y, x, gate2, final_g, final):
    bn, n, _ = x.shape
    tn = COMBINE_TN
    tok = lambda b, i: (b, i, 0)
    return pl.pallas_call(
        functools.partial(_combine_kernel, final=final),
        grid=(bn, n // tn),
        in_specs=[pl.BlockSpec((1, tn, LANES), tok),
                  pl.BlockSpec((N_EXPERTS, CAP, D_MODEL), lambda b, i: (0, b, 0)),
                  pl.BlockSpec((1, tn, D_MODEL), tok),
                  pl.BlockSpec((1, 1, D_MODEL), lambda b, i: (b, 0, 0)),
                  pl.BlockSpec((1, D_MODEL), lambda b, i: (0, 0))],
        out_specs=pl.BlockSpec((1, tn, D_MODEL), tok),
        out_shape=jax.ShapeDtypeStruct((bn, n, D_MODEL), F32),
        compiler_params=_params("parallel", "arbitrary"),
        name="moe_combine",
    )(post, y, x, gate2, final_g.reshape(1, -1))


def _conv_proj_kernel(x_ref, g_ref, sh_ref, sc_ref, w_ref, v_ref, gb_ref, u_ref):
    h = _norm_mod(x_ref[0], g_ref[...], sh_ref[0], sc_ref[0]).astype(BF16)
    w = CONV_W
    xc = _dot(h, w_ref[:, 0 * w:1 * w])
    gate_c = _dot(h, w_ref[:, 2 * w:3 * w])
    v_ref[0] = gate_c * xc
    gb_ref[0] = _dot(h, w_ref[:, 1 * w:2 * w])
    glu_a = _dot(h, w_ref[:, 3 * w:4 * w])
    glu_g = _dot(h, w_ref[:, 4 * w:5 * w])
    u_ref[0] = glu_a * _sigmoid(glu_g)


def _conv_proj(x, g, shift, scale, w):
    bn, n, _ = x.shape
    tn = PROJ_TN
    tok = lambda b, i: (b, i, 0)
    per_b = lambda b, i: (b, 0, 0)
    narrow = pl.BlockSpec((1, tn, CONV_W), tok)
    vec = pl.BlockSpec((1, 1, D_MODEL), per_b)
    return pl.pallas_call(
        _conv_proj_kernel,
        grid=(bn, n // tn),
        in_specs=[pl.BlockSpec((1, tn, D_MODEL), tok),
                  pl.BlockSpec((1, D_MODEL), lambda b, i: (0, 0)),
                  vec, vec,
                  pl.BlockSpec((D_MODEL, CONV_IN), lambda b, i: (0, 0))],
        out_specs=[narrow, narrow, narrow],
        out_shape=[jax.ShapeDtypeStruct((bn, n, CONV_W), F32)] * 3,
        compiler_params=_params("parallel", "parallel"),
        name="conv_proj",
    )(x, g.reshape(1, -1), shift, scale, w)


def _depthwise(win, w_ref, cols, taps, rows):
    offs = [CONV_HALO + k - taps // 2 for k in range(taps)]
    acc = None
    for b in range(8):
        ks = [k for k in range(taps) if offs[k] % 8 == b]
        if not ks:
            continue
        shifted = pltpu.roll(win, win.shape[0] - b, 0) if b else win
        for k in ks:
            term = shifted[offs[k] - b:offs[k] - b + rows, :] * w_ref[k:k + 1, cols]
            acc = term if acc is None else acc + term
    return acc


def _conv_kernel(v_ref, gb_ref, u_ref, scw_ref, cfw_ref, cfb_ref, lng_ref, lnb_ref, yc_ref, yd_ref, vpad_ref, upad_ref):
    i = pl.program_id(1)
    n = v_ref.shape[1]
    tn = gb_ref.shape[1]

    @pl.when(i == 0)
    def _stage():
        zeros = jnp.zeros((CONV_HALO, CONV_W), F32)
        for pad_ref, src_ref in ((vpad_ref, v_ref), (upad_ref, u_ref)):
            pad_ref[0:CONV_HALO, :] = zeros
            pad_ref[CONV_HALO:CONV_HALO + n, :] = src_ref[0]
            pad_ref[CONV_HALO + n:, :] = zeros

    for j in range(tn // CONV_SUB):
        rows = slice(j * CONV_SUB, (j + 1) * CONV_SUB)
        start = pl.multiple_of(i * tn + j * CONV_SUB, CONV_SUB)
        yd_parts = []
        for s in range(CONV_W // LANES):
            cols = slice(s * LANES, (s + 1) * LANES)
            vwin = vpad_ref[pl.ds(start, CONV_SUB + 2 * CONV_HALO), cols]
            yc = gb_ref[0, rows, cols] * _depthwise(vwin, scw_ref, cols, SC_K, CONV_SUB)
            yc_ref[0, rows, cols] = yc.astype(BF16)
            uwin = upad_ref[pl.ds(start, CONV_SUB + 2 * CONV_HALO), cols]
            yd_parts.append(_depthwise(uwin, cfw_ref, cols, CF_K, CONV_SUB) + cfb_ref[:, cols])
        u = jnp.concatenate(yd_parts, axis=-1)
        mu = jnp.mean(u, axis=-1, keepdims=True)
        d = u - mu
        var = jnp.mean(d * d, axis=-1, keepdims=True)
        ln = (d * lax.rsqrt(var + EPS)) * lng_ref[...] + lnb_ref[...]
        yd_ref[0, rows, :] = (ln * _sigmoid(ln)).astype(BF16)


def _convs(v, gb, u, sc_w, cf_w, cf_b, ln_g, ln_b):
    bn, n, w = v.shape
    tn = CONV_TN
    full = pl.BlockSpec((1, n, w), lambda b, i: (b, 0, 0))
    tile = pl.BlockSpec((1, tn, w), lambda b, i: (b, i, 0))
    vec = pl.BlockSpec((1, w), lambda b, i: (0, 0))
    return pl.pallas_call(
        _conv_kernel,
        grid=(bn, n // tn),
        in_specs=[full, tile, full,
                  pl.BlockSpec((SC_K, w), lambda b, i: (0, 0)),
                  pl.BlockSpec((CF_K, w), lambda b, i: (0, 0)),
                  vec, vec, vec],
        out_specs=[tile, tile],
        out_shape=[jax.ShapeDtypeStruct((bn, n, w), BF16)] * 2,
        scratch_shapes=[pltpu.VMEM((n + 2 * CONV_HALO, w), F32)] * 2,
        compiler_params=_params("parallel", "arbitrary"),
        name="convs",
    )(v, gb, u, sc_w, cf_w, cf_b.reshape(1, -1), ln_g.reshape(1, -1), ln_b.reshape(1, -1))


def _split_mod(mod, bn):
    return [mod[:bn, j * D_MODEL:(j + 1) * D_MODEL].reshape(bn, 1, D_MODEL) for j in range(6)]


def _moe(x, gate2, aff_t, h2, w1, w3, w2, final_g, final):
    pos, gsel, post = _route(aff_t)
    xg, gs = _gather(pos, gsel, h2)
    y = _expert_ffn(xg, gs, w1, w3, w2)
    return _combine(post, y, x, gate2, final_g, final)


def kernel(x, c, ctx, c_ctx, l0_norm1, l0_w_mod, l0_b_mod, l0_w_in, l0_w_out, l0_rpb, l0_lam_q1, l0_lam_k1, l0_lam_q2, l0_lam_k2, l0_subln, l0_norm2, l0_w_router, l0_w1, l0_w3, l0_w2, l1_norm1, l1_w_mod, l1_b_mod, l1_w_in, l1_w_out, l1_sc_w, l1_cf_w, l1_cf_b, l1_ln_g, l1_ln_b, l1_norm2, l1_w_router, l1_w1, l1_w3, l1_w2, final_norm):
    bn = x.shape[0]
    pad_rows = -(bn + 1) % 8
    cc = jnp.concatenate([c, c_ctx[None, :], jnp.zeros((pad_rows, D_MODEL), F32)], axis=0)

    mod0 = _modulation(cc, l0_w_mod, l0_b_mod)
    shift1, scale1, gate1, shift2, scale2, gate2 = _split_mod(mod0, bn)
    ctx_shift = jnp.broadcast_to(mod0[bn, :D_MODEL].reshape(1, 1, D_MODEL), (bn, 1, D_MODEL))
    ctx_scale = jnp.broadcast_to(mod0[bn, D_MODEL:2 * D_MODEL].reshape(1, 1, D_MODEL), (bn, 1, D_MODEL))
    w_in0 = l0_w_in.astype(BF16)
    cos_t, sin_t = _rope_tables()
    z = _attn_proj(x, l0_norm1, shift1, scale1, w_in0, cos_t, sin_t, rope_groups=(1, 3), scaled_groups=(0, 1))
    zc = _attn_proj(ctx, l0_norm1, ctx_shift, ctx_scale, w_in0[:, 2 * HEAD_GROUP:], cos_t[:CTX_LEN], sin_t[:CTX_LEN],
                    rope_groups=(), scaled_groups=())
    y_a = _na_attention(z, zc, _na_bias_table(l0_rpb))
    y_b = _diff_attention(z, zc, l0_lam_q1, l0_lam_k1, l0_lam_q2, l0_lam_k2, l0_subln)
    x1, h2, aff_t = _out_proj(x, y_a, y_b, l0_w_out.astype(BF16), gate1, l0_norm2, shift2, scale2,
                              l0_w_router.T.astype(BF16))
    x2 = _moe(x1, gate2, aff_t, h2, l0_w1, l0_w3, l0_w2, final_norm, final=False)

    mod1 = _modulation(cc, l1_w_mod, l1_b_mod)
    shift1b, scale1b, gate1b, shift2b, scale2b, gate2b = _split_mod(mod1, bn)
    v, gb, u = _conv_proj(x2, l1_norm1, shift1b, scale1b, l1_w_in.astype(BF16))
    y_c, y_d = _convs(v, gb, u, l1_sc_w, l1_cf_w, l1_cf_b, l1_ln_g, l1_ln_b)
    x3, h2b, aff_tb = _out_proj(x2, y_c, y_d, l1_w_out.astype(BF16), gate1b, l1_norm2, shift2b, scale2b,
                                l1_w_router.T.astype(BF16))
    return _moe(x3, gate2b, aff_tb, h2b, l1_w1, l1_w3, l1_w2, final_norm, final=True)
```

```python
import functools
import math

import numpy as np
import jax
import jax.numpy as jnp
from jax import lax
from jax.experimental import pallas as pl
from jax.experimental.pallas import tpu as pltpu

D_MODEL = 1024
BATCH = 16
SEQ = 2048
GRID_W = 64
GRID_H = SEQ // GRID_W
CTX_LEN = 256
NA_HEADS = 8
NA_WIN_H = 8
NA_WIN_W = 16
DIFF_HEADS = 4
DIFF_DIM = 64
HEAD_GROUP = 512
ATTN_IN = 6 * HEAD_GROUP
ROPE_THETA = 10000.0
ROPE_FREQS = DIFF_DIM // 4
SC_K = 3
CF_K = 31
CONV_W = 512
CONV_IN = 5 * CONV_W
N_EXPERTS = 16
D_EXPERT = 2816
CAP = 2 * SEQ // N_EXPERTS
EPS = 1e-6
LAM_INIT0 = 0.8 - 0.6 * math.exp(-0.3 * 0)
NEG = -1e30
LOG2E = math.log2(math.e)
Q_SCALE = DIFF_DIM ** -0.5 * LOG2E

LANES = 128
NA_ROWS = 4
NA_BAND = 12
PROJ_TN = 1024
DIFF_TQ = 512
CONV_TN = 512
CONV_HALO = 16
CONV_SUB = 128
ROUTE_B = 8
FFN_TF = 256
FFN_M_SPLIT = 2
FFN_ROW_CHUNKS = 2
COMBINE_TN = 1024
VMEM_LIMIT = 60 * 1024 * 1024

F32 = jnp.float32
BF16 = jnp.bfloat16


def _dot(a, b):
    return jnp.dot(a, b, preferred_element_type=F32)


def _dot_nt(a, b):
    return lax.dot_general(a, b, (((1,), (1,)), ((), ())), preferred_element_type=F32)


def _sigmoid(x):
    return 1.0 / (1.0 + jnp.exp(-x))


def _params(*sem):
    return pltpu.CompilerParams(dimension_semantics=sem, vmem_limit_bytes=VMEM_LIMIT)


def _norm_mod(x, g, shift, scale):
    ms = jnp.mean(x * x, axis=-1, keepdims=True)
    return (x * lax.rsqrt(ms + EPS)) * g * (1.0 + scale) + shift


def _mod_kernel(c_ref, w_ref, b_ref, o_ref):
    cc = c_ref[...]
    s = cc * _sigmoid(cc)
    o_ref[...] = _dot(s.astype(BF16), w_ref[...].astype(BF16)) + b_ref[...]


def _modulation(cc, w_mod, b_mod):
    rows = cc.shape[0]
    tc = D_MODEL
    return pl.pallas_call(
        _mod_kernel,
        grid=(w_mod.shape[1] // tc,),
        in_specs=[pl.BlockSpec((rows, D_MODEL), lambda j: (0, 0)),
                  pl.BlockSpec((D_MODEL, tc), lambda j: (0, j)),
                  pl.BlockSpec((1, tc), lambda j: (0, j))],
        out_specs=pl.BlockSpec((rows, tc), lambda j: (0, j)),
        out_shape=jax.ShapeDtypeStruct((rows, w_mod.shape[1]), F32),
        compiler_params=_params("arbitrary"),
        name="modulation",
    )(cc, w_mod, b_mod.reshape(1, -1))


def _rope(z, cos, sin):
    lane = lax.broadcasted_iota(jnp.int32, z.shape, 1)
    partner = jnp.where((lane & (2 * ROPE_FREQS - 1)) < ROPE_FREQS,
                        pltpu.roll(z, LANES - ROPE_FREQS, 1), pltpu.roll(z, ROPE_FREQS, 1))
    return z * cos + partner * sin


def _attn_proj_kernel(x_ref, g_ref, sh_ref, sc_ref, w_ref, cos_ref, sin_ref, o_ref, *, rope_groups, scaled_groups):
    h = _norm_mod(x_ref[0], g_ref[...], sh_ref[0], sc_ref[0]).astype(BF16)
    n_groups = w_ref.shape[1] // HEAD_GROUP
    for j in range(n_groups):
        z = _dot(h, w_ref[:, j * HEAD_GROUP:(j + 1) * HEAD_GROUP])
        for s in range(HEAD_GROUP // LANES):
            zz = z[:, s * LANES:(s + 1) * LANES]
            if j in rope_groups:
                zz = _rope(zz, cos_ref[...], sin_ref[...])
            if j in scaled_groups:
                zz = zz * Q_SCALE
            o_ref[0, :, j * HEAD_GROUP + s * LANES:j * HEAD_GROUP + (s + 1) * LANES] = zz.astype(BF16)


def _attn_proj(x, g, shift, scale, w, cos_t, sin_t, rope_groups, scaled_groups):
    bn, n, _ = x.shape
    tn = min(PROJ_TN, n)
    cols = w.shape[1]
    return pl.pallas_call(
        functools.partial(_attn_proj_kernel, rope_groups=rope_groups, scaled_groups=scaled_groups),
        grid=(bn, n // tn),
        in_specs=[pl.BlockSpec((1, tn, D_MODEL), lambda b, i: (b, i, 0)),
                  pl.BlockSpec((1, D_MODEL), lambda b, i: (0, 0)),
                  pl.BlockSpec((1, 1, D_MODEL), lambda b, i: (b, 0, 0)),
                  pl.BlockSpec((1, 1, D_MODEL), lambda b, i: (b, 0, 0)),
                  pl.BlockSpec((D_MODEL, cols), lambda b, i: (0, 0)),
                  pl.BlockSpec((tn, LANES), lambda b, i: (i, 0)),
                  pl.BlockSpec((tn, LANES), lambda b, i: (i, 0))],
        out_specs=pl.BlockSpec((1, tn, cols), lambda b, i: (b, i, 0)),
        out_shape=jax.ShapeDtypeStruct((bn, n, cols), BF16),
        compiler_params=_params("parallel", "parallel"),
        name="attn_proj",
    )(x, g.reshape(1, -1), shift, scale, w, cos_t, sin_t)


def _rope_tables():
    n_freq = ROPE_FREQS
    freqs = ROPE_THETA ** (-jnp.arange(n_freq, dtype=F32) / n_freq)
    t = jnp.arange(SEQ)
    row, col = t // GRID_W, t % GRID_W

    def half(pos):
        ang = pos.astype(F32)[:, None] * freqs
        c, s = jnp.cos(ang), jnp.sin(ang)
        return jnp.concatenate([c, c], axis=-1), jnp.concatenate([-s, s], axis=-1)

    cr, sr = half(row)
    cc, sc = half(col)
    cos64 = jnp.concatenate([cr, cc], axis=-1)
    sin64 = jnp.concatenate([sr, sc], axis=-1)
    return jnp.tile(cos64, (1, LANES // DIFF_DIM)), jnp.tile(sin64, (1, LANES // DIFF_DIM))


def _na_bias_table(rpb):
    n_dr, n_dc = 2 * NA_WIN_H - 1, 2 * NA_WIN_W - 1
    c = np.arange(GRID_W)
    cs = np.clip(c - NA_WIN_W // 2, 0, GRID_W - NA_WIN_W)
    kc = np.arange(GRID_W)
    col_ok = (kc[None, :] >= cs[:, None]) & (kc[None, :] < cs[:, None] + NA_WIN_W)
    dc = kc[None, :] - c[:, None] + NA_WIN_W - 1
    one_hot = (dc[None] == np.arange(n_dc)[:, None, None]) & col_ok[None]
    pair_hot = np.zeros((2, n_dc, GRID_W, 2, GRID_W), np.float32)
    for k2 in range(2):
        pair_hot[k2, :, :, k2, :] = one_hot
    pair_hot = pair_hot.reshape(2 * n_dc, GRID_W, 2 * GRID_W)
    padded = jnp.pad(rpb.astype(F32) * LOG2E, ((0, 0), (1, 1), (0, 0)))
    pairs = jnp.concatenate([padded[:, :-1], padded[:, 1:]], axis=-1)
    slabs = jnp.einsum('hsd,dcn->hscn', pairs, jnp.asarray(pair_hot), precision=lax.Precision.HIGHEST)
    blocks = np.array([0, 2 * NA_ROWS, GRID_H - NA_ROWS])
    starts = np.clip(blocks - NA_WIN_H // 2, 0, GRID_H - NA_BAND)
    r = blocks[:, None] + np.arange(NA_ROWS)[None, :]
    rs = np.clip(r - NA_WIN_H // 2, 0, GRID_H - NA_WIN_H)
    krow = starts[:, None, None] + np.arange(NA_BAND)[None, None, :]
    row_ok = (krow >= rs[:, :, None]) & (krow < rs[:, :, None] + NA_WIN_H)
    dr_even = krow[:, :, 0::2] - r[:, :, None] + NA_WIN_H - 1
    slab_idx = np.clip(dr_even + 1, 0, n_dr)
    picked = jnp.take(slabs, jnp.asarray(slab_idx.reshape(-1)), axis=1)
    picked = picked.reshape(NA_HEADS, len(blocks), NA_ROWS, NA_BAND // 2, GRID_W, 2 * GRID_W)
    row_ok_lanes = np.repeat(row_ok.reshape(len(blocks), NA_ROWS, NA_BAND // 2, 2), GRID_W, axis=-1)
    ok = row_ok_lanes[None, :, :, :, None, :] & np.tile(col_ok, (1, 2))[None, None, None, None, :, :]
    table = jnp.where(jnp.asarray(ok), picked, NEG).transpose(1, 0, 3, 2, 4, 5)
    return table.reshape(len(blocks), NA_HEADS, NA_BAND // 2, NA_ROWS * GRID_W, 2 * GRID_W)


def _na_band_start(i):
    return jnp.clip(i * NA_ROWS - NA_WIN_H // 2, 0, GRID_H - NA_BAND)


def _na_kernel(q_ref, k_ref, v_ref, kc_ref, vc_ref, bias_ref, o_ref):
    start = pl.multiple_of(_na_band_start(pl.program_id(1)) * GRID_W, GRID_W)
    band = NA_BAND * GRID_W
    lane = lax.broadcasted_iota(jnp.int32, (q_ref.shape[1], LANES), 1)
    first = lane < (LANES // 2)
    for p in range(NA_HEADS // 2):
        cols = slice(p * LANES, (p + 1) * LANES)
        q2 = q_ref[0, :, cols]
        kb = k_ref[0, pl.ds(start, band), cols]
        kc = kc_ref[0, :, cols]
        vb = jnp.concatenate([v_ref[0, pl.ds(start, band), cols], jnp.ones((band, LANES), BF16)], axis=-1)
        vc = jnp.concatenate([vc_ref[0, :, cols], jnp.ones((kc.shape[0], LANES), BF16)], axis=-1)
        outs = []
        for par in range(2):
            qm = jnp.where(first if par == 0 else jnp.logical_not(first), q2, jnp.zeros_like(q2))
            s_w = _dot_nt(qm, kb)
            s_w = jnp.concatenate([s_w[:, m * LANES:(m + 1) * LANES] + bias_ref[0, 2 * p + par, m]
                                   for m in range(NA_BAND // 2)], axis=-1)
            s_c = _dot_nt(qm, kc)
            m = jnp.maximum(jnp.max(s_w, axis=-1, keepdims=True), jnp.max(s_c, axis=-1, keepdims=True))
            p_w = jnp.exp2(s_w - m)
            p_c = jnp.exp2(s_c - m)
            o = _dot(p_w.astype(BF16), vb) + _dot(p_c.astype(BF16), vc)
            outs.append(o[:, :LANES] / o[:, LANES:])
        o_ref[0, :, cols] = jnp.where(first, outs[0], outs[1]).astype(BF16)


def _na_block_class(i):
    return (i * NA_ROWS - _na_band_start(i)) // NA_ROWS


def _na_attention(z, zc, bias):
    bn = z.shape[0]
    tq = NA_ROWS * GRID_W
    return pl.pallas_call(
        _na_kernel,
        grid=(bn, GRID_H // NA_ROWS),
        in_specs=[pl.BlockSpec((1, tq, HEAD_GROUP), lambda b, i: (b, i, 0)),
                  pl.BlockSpec((1, SEQ, HEAD_GROUP), lambda b, i: (b, 0, 2)),
                  pl.BlockSpec((1, SEQ, HEAD_GROUP), lambda b, i: (b, 0, 4)),
                  pl.BlockSpec((1, CTX_LEN, HEAD_GROUP), lambda b, i: (b, 0, 0)),
                  pl.BlockSpec((1, CTX_LEN, HEAD_GROUP), lambda b, i: (b, 0, 2)),
                  pl.BlockSpec((1, NA_HEADS, NA_BAND // 2, tq, 2 * GRID_W),
                               lambda b, i: (_na_block_class(i), 0, 0, 0, 0))],
        out_specs=pl.BlockSpec((1, tq, HEAD_GROUP), lambda b, i: (b, i, 0)),
        out_shape=jax.ShapeDtypeStruct((bn, SEQ, HEAD_GROUP), BF16),
        compiler_params=_params("parallel", "arbitrary"),
        name="na_attention",
    )(z, z, z, zc, zc, bias)


def _diff_kernel(lq1_ref, lk1_ref, lq2_ref, lk2_ref, sub_ref, q_ref, k_ref, v_ref, kc_ref, vc_ref, o_ref):
    lam = (jnp.exp(jnp.sum(lq1_ref[...] * lk1_ref[...], axis=-1, keepdims=True))
           - jnp.exp(jnp.sum(lq2_ref[...] * lk2_ref[...], axis=-1, keepdims=True)) + LAM_INIT0)
    tq = q_ref.shape[1]
    lane = lax.broadcasted_iota(jnp.int32, (tq, LANES), 1)
    first = lane < DIFF_DIM

    def scores(h):
        cols = slice(h * LANES, (h + 1) * LANES)
        q2 = q_ref[0, :, cols]
        out = []
        for mp in range(2):
            qm = jnp.where(first if mp == 0 else jnp.logical_not(first), q2, jnp.zeros_like(q2))
            out.append((_dot_nt(qm, k_ref[0, :, cols]), _dot_nt(qm, kc_ref[0, :, cols])))
        return out

    nxt = scores(0)
    for h in range(DIFF_HEADS):
        cols = slice(h * LANES, (h + 1) * LANES)
        cur = nxt
        if h + 1 < DIFF_HEADS:
            nxt = scores(h + 1)
        v_l = jnp.concatenate([v_ref[0, :, cols], jnp.ones((v_ref.shape[1], LANES), BF16)], axis=-1)
        v_c = jnp.concatenate([vc_ref[0, :, cols], jnp.ones((vc_ref.shape[1], LANES), BF16)], axis=-1)
        maps = []
        for s_l, s_c in cur:
            m = jnp.maximum(jnp.max(s_l, axis=-1, keepdims=True), jnp.max(s_c, axis=-1, keepdims=True))
            pv = _dot(jnp.exp2(s_l - m).astype(BF16), v_l) + _dot(jnp.exp2(s_c - m).astype(BF16), v_c)
            maps.append(pv[:, :LANES] / pv[:, LANES:])
        o = maps[0] - lam * maps[1]
        ms = jnp.mean(o * o, axis=-1, keepdims=True)
        o = (o * lax.rsqrt(ms + EPS)) * sub_ref[...] * (1.0 - LAM_INIT0)
        o_ref[0, :, cols] = o.astype(BF16)


def _diff_attention(z, zc, lq1, lk1, lq2, lk2, subln):
    bn = z.shape[0]
    tq = DIFF_TQ
    vec = pl.BlockSpec((1, DIFF_DIM), lambda b, i: (0, 0))
    return pl.pallas_call(
        _diff_kernel,
        grid=(bn, SEQ // tq),
        in_specs=[vec, vec, vec, vec,
                  pl.BlockSpec((1, 2 * DIFF_DIM), lambda b, i: (0, 0)),
                  pl.BlockSpec((1, tq, HEAD_GROUP), lambda b, i: (b, i, 1)),
                  pl.BlockSpec((1, SEQ, HEAD_GROUP), lambda b, i: (b, 0, 3)),
                  pl.BlockSpec((1, SEQ, HEAD_GROUP), lambda b, i: (b, 0, 5)),
                  pl.BlockSpec((1, CTX_LEN, HEAD_GROUP), lambda b, i: (b, 0, 1)),
                  pl.BlockSpec((1, CTX_LEN, HEAD_GROUP), lambda b, i: (b, 0, 3))],
        out_specs=pl.BlockSpec((1, tq, HEAD_GROUP), lambda b, i: (b, i, 0)),
        out_shape=jax.ShapeDtypeStruct((bn, SEQ, HEAD_GROUP), BF16),
        compiler_params=_params("parallel", "arbitrary"),
        name="diff_attention",
    )(lq1.reshape(1, -1), lk1.reshape(1, -1), lq2.reshape(1, -1), lk2.reshape(1, -1), subln.reshape(1, -1),
      z, z, z, zc, zc)


def _out_kernel(x_ref, ya_ref, yb_ref, w_ref, g1_ref, n2_ref, sh2_ref, sc2_ref, wr_ref, x1_ref, h2_ref, aff_ref):
    half = ya_ref.shape[2]
    y = _dot(ya_ref[0], w_ref[:half, :]) + _dot(yb_ref[0], w_ref[half:, :])
    x1 = x_ref[0] + g1_ref[0] * y
    x1_ref[0] = x1
    hb = _norm_mod(x1, n2_ref[...], sh2_ref[0], sc2_ref[0]).astype(BF16)
    h2_ref[0] = hb
    logits = _dot_nt(wr_ref[...], hb)
    m = jnp.max(logits, axis=0, keepdims=True)
    p = jnp.exp(logits - m)
    aff_ref[0] = p / jnp.sum(p, axis=0, keepdims=True)


def _out_proj(x, ya, yb, w_out, gate1, norm2, shift2, scale2, w_router_t):
    bn, n, _ = x.shape
    tn = PROJ_TN
    half = ya.shape[2]
    tok = lambda b, i: (b, i, 0)
    per_b = lambda b, i: (b, 0, 0)
    return pl.pallas_call(
        _out_kernel,
        grid=(bn, n // tn),
        in_specs=[pl.BlockSpec((1, tn, D_MODEL), tok),
                  pl.BlockSpec((1, tn, half), tok),
                  pl.BlockSpec((1, tn, half), tok),
                  pl.BlockSpec((2 * half, D_MODEL), lambda b, i: (0, 0)),
                  pl.BlockSpec((1, 1, D_MODEL), per_b),
                  pl.BlockSpec((1, D_MODEL), lambda b, i: (0, 0)),
                  pl.BlockSpec((1, 1, D_MODEL), per_b),
                  pl.BlockSpec((1, 1, D_MODEL), per_b),
                  pl.BlockSpec((N_EXPERTS, D_MODEL), lambda b, i: (0, 0))],
        out_specs=[pl.BlockSpec((1, tn, D_MODEL), tok),
                   pl.BlockSpec((1, tn, D_MODEL), tok),
                   pl.BlockSpec((1, N_EXPERTS, tn), lambda b, i: (b, 0, i))],
        out_shape=[jax.ShapeDtypeStruct((bn, n, D_MODEL), F32),
                   jax.ShapeDtypeStruct((bn, n, D_MODEL), BF16),
                   jax.ShapeDtypeStruct((bn, N_EXPERTS, n), F32)],
        compiler_params=_params("parallel", "parallel"),
        name="out_proj",
    )(x, ya, yb, w_out, gate1, norm2.reshape(1, -1), shift2, scale2, w_router_t)


def _excl_cumsum_lanes(x01):
    e, n = x01.shape
    i = lax.broadcasted_iota(jnp.int32, (LANES, LANES), 0)
    j = lax.broadcasted_iota(jnp.int32, (LANES, LANES), 1)
    tri = jnp.where(i < j, 1.0, 0.0).astype(BF16)
    offset = jnp.zeros((e, 1), F32)
    chunks = []
    for c in range(n // LANES):
        xc = x01[:, c * LANES:(c + 1) * LANES]
        chunks.append(_dot(xc.astype(BF16), tri) + offset)
        offset = offset + jnp.sum(xc, axis=-1, keepdims=True)
    return jnp.concatenate(chunks, axis=-1), offset


def _route_kernel(aff_ref, pos_ref, g_ref, post_ref):
    nb, ne, n = aff_ref.shape
    a = aff_ref[...].reshape(nb * ne, n)
    thr = jnp.zeros((a.shape[0], 1), jnp.int32)
    for bit in range(30, -1, -1):
        cand = thr | (1 << bit)
        cnt = jnp.sum(jnp.where(a >= lax.bitcast_convert_type(cand, F32), 1.0, 0.0), axis=-1, keepdims=True)
        thr = jnp.where(cnt >= CAP, cand, thr)
    gt = jnp.where(a >= lax.bitcast_convert_type(thr + 1, F32), 1.0, 0.0)
    eq = jnp.where(a >= lax.bitcast_convert_type(thr, F32), 1.0, 0.0) - gt
    need = CAP - jnp.sum(gt, axis=-1, keepdims=True)
    eq_rank, _ = _excl_cumsum_lanes(eq)
    sel = gt + eq * jnp.where(eq_rank < need, 1.0, 0.0)
    pos, _ = _excl_cumsum_lanes(sel)
    pos = jnp.where(sel > 0.5, pos, -1.0)
    pos_ref[...] = pos.astype(jnp.int32).reshape(nb, ne, n)
    g_ref[...] = jnp.where(sel > 0.5, a, 0.0).reshape(nb, ne, n)
    fill = jnp.full((LANES - ne, n), -1.0, F32)
    for b in range(nb):
        padded = jnp.concatenate([pos[b * ne:(b + 1) * ne, :], fill], axis=0)
        post_ref[b] = padded.T.astype(jnp.int32)


def _route(aff_t):
    bn, e, n = aff_t.shape
    blk = pl.BlockSpec((ROUTE_B, e, n), lambda b: (b, 0, 0))
    return pl.pallas_call(
        _route_kernel,
        grid=(bn // ROUTE_B,),
        in_specs=[blk],
        out_specs=[blk, blk, pl.BlockSpec((ROUTE_B, n, LANES), lambda b: (b, 0, 0))],
        out_shape=[jax.ShapeDtypeStruct((bn, e, n), jnp.int32), jax.ShapeDtypeStruct((bn, e, n), F32),
                   jax.ShapeDtypeStruct((bn, n, LANES), jnp.int32)],
        compiler_params=_params("parallel"),
        name="route",
    )(aff_t)


def _one_hot_slots(pos_row):
    slot = lax.broadcasted_iota(jnp.int32, (CAP, pos_row.shape[1]), 0)
    return slot == pos_row


def _gather_kernel(pos_ref, g_ref, h_ref, x_ref, gs_ref):
    for e in range(N_EXPERTS):
        hit = _one_hot_slots(pos_ref[0, e:e + 1, :])
        x_ref[e] = _dot(jnp.where(hit, 1.0, 0.0).astype(BF16), h_ref[0]).astype(BF16)
        g = jnp.sum(jnp.where(hit, g_ref[0, e:e + 1, :], 0.0), axis=-1, keepdims=True)
        gs_ref[e] = jnp.broadcast_to(g, (CAP, LANES))


def _gather(pos, gsel, h2):
    bn, e, n = pos.shape
    rows = pl.BlockSpec((1, e, n), lambda b: (b, 0, 0))
    return pl.pallas_call(
        _gather_kernel,
        grid=(bn,),
        in_specs=[rows, rows, pl.BlockSpec((1, n, D_MODEL), lambda b: (b, 0, 0))],
        out_specs=[pl.BlockSpec((e, CAP, D_MODEL), lambda b: (0, b, 0)),
                   pl.BlockSpec((e, CAP, LANES), lambda b: (0, b, 0))],
        out_shape=[jax.ShapeDtypeStruct((e, bn * CAP, D_MODEL), BF16),
                   jax.ShapeDtypeStruct((e, bn * CAP, LANES), F32)],
        compiler_params=_params("parallel"),
        name="moe_gather",
    )(pos, gsel, h2)


def _ffn_kernel(x_ref, gs_ref, w1_ref, w3_ref, w2_ref, y_ref, acc_ref):
    f = pl.program_id(2)
    n_f = pl.num_programs(2)

    def step(first, last):
        w1 = w1_ref[0].astype(BF16)
        w3 = w3_ref[0].astype(BF16)
        w2 = w2_ref[0].astype(BF16)
        chunk = x_ref.shape[1] // FFN_ROW_CHUNKS
        for c in range(FFN_ROW_CHUNKS):
            rows = slice(c * chunk, (c + 1) * chunk)
            xs = x_ref[0, rows, :]
            a = _dot(xs, w1)
            hid = (a * _sigmoid(a)) * _dot(xs, w3)
            y = _dot(hid.astype(BF16), w2)
            if first:
                acc_ref[rows, :] = y
            elif last:
                g = gs_ref[0, rows, :]
                for j in range(D_MODEL // LANES):
                    cols = slice(j * LANES, (j + 1) * LANES)
                    y_ref[0, rows, cols] = ((acc_ref[rows, cols] + y[:, cols]) * g).astype(BF16)
            else:
                acc_ref[rows, :] += y

    pl.when(f == 0)(functools.partial(step, True, False))
    pl.when(jnp.logical_and(f > 0, f < n_f - 1))(functools.partial(step, False, False))
    pl.when(f == n_f - 1)(functools.partial(step, False, True))


def _expert_ffn(xg, gs, w1, w3, w2):
    e, rows, _ = xg.shape
    tm = rows // FFN_M_SPLIT
    n_f = D_EXPERT // FFN_TF
    return pl.pallas_call(
        _ffn_kernel,
        grid=(e, FFN_M_SPLIT, n_f),
        in_specs=[pl.BlockSpec((1, tm, D_MODEL), lambda ex, m, f: (ex, m, 0)),
                  pl.BlockSpec((1, tm, LANES), lambda ex, m, f: (ex, m, 0)),
                  pl.BlockSpec((1, D_MODEL, FFN_TF), lambda ex, m, f: (ex, 0, f)),
                  pl.BlockSpec((1, D_MODEL, FFN_TF), lambda ex, m, f: (ex, 0, f)),
                  pl.BlockSpec((1, FFN_TF, D_MODEL), lambda ex, m, f: (ex, f, 0))],
        out_specs=pl.BlockSpec((1, tm, D_MODEL), lambda ex, m, f: (ex, m, 0)),
        out_shape=jax.ShapeDtypeStruct((e, rows, D_MODEL), BF16),
        scratch_shapes=[pltpu.VMEM((tm, D_MODEL), F32)],
        compiler_params=_params("parallel", "parallel", "arbitrary"),
        name="expert_ffn",
    )(xg, gs, w1, w3, w2)


def _combine_kernel(post_ref, y_ref, x_ref, g2_ref, fin_ref, o_ref, *, final):
    tn = post_ref.shape[1]
    slot = lax.broadcasted_iota(jnp.int32, (tn, CAP), 1)
    hits = [jnp.where(post_ref[0, :, e:e + 1] == slot, 1.0, 0.0).astype(BF16) for e in range(N_EXPERTS)]
    moe = _dot(jnp.concatenate(hits, axis=-1), y_ref[...].reshape(N_EXPERTS * CAP, D_MODEL))
    x = x_ref[0] + g2_ref[0] * moe
    if final:
        ms = jnp.mean(x * x, axis=-1, keepdims=True)
        x = (x * lax.rsqrt(ms + EPS)) * fin_ref[...]
    o_ref[0] = x


def _combine(post, y, x, gate2, final_g, final):
    bn, n, _ = x.shape
    tn = COMBINE_TN
    tok = lambda b, i: (b, i, 0)
    return pl.pallas_call(
        functools.partial(_combine_kernel, final=final),
        grid=(bn, n // tn),
        in_specs=[pl.BlockSpec((1, tn, LANES), tok),
                  pl.BlockSpec((N_EXPERTS, CAP, D_MODEL), lambda b, i: (0, b, 0)),
                  pl.BlockSpec((1, tn, D_MODEL), tok),
                  pl.BlockSpec((1, 1, D_MODEL), lambda b, i: (b, 0, 0)),
                  pl.BlockSpec((1, D_MODEL), lambda b, i: (0, 0))],
        out_specs=pl.BlockSpec((1, tn, D_MODEL), tok),
        out_shape=jax.ShapeDtypeStruct((bn, n, D_MODEL), F32),
        compiler_params=_params("parallel", "arbitrary"),
        name="moe_combine",
    )(post, y, x, gate2, final_g.reshape(1, -1))


def _conv_proj_kernel(x_ref, g_ref, sh_ref, sc_ref, w_ref, v_ref, gb_ref, u_ref):
    h = _norm_mod(x_ref[0], g_ref[...], sh_ref[0], sc_ref[0]).astype(BF16)
    w = CONV_W
    xc = _dot(h, w_ref[:, 0 * w:1 * w])
    gate_c = _dot(h, w_ref[:, 2 * w:3 * w])
    v_ref[0] = gate_c * xc
    gb_ref[0] = _dot(h, w_ref[:, 1 * w:2 * w])
    glu_a = _dot(h, w_ref[:, 3 * w:4 * w])
    glu_g = _dot(h, w_ref[:, 4 * w:5 * w])
    u_ref[0] = glu_a * _sigmoid(glu_g)


def _conv_proj(x, g, shift, scale, w):
    bn, n, _ = x.shape
    tn = PROJ_TN
    tok = lambda b, i: (b, i, 0)
    per_b = lambda b, i: (b, 0, 0)
    narrow = pl.BlockSpec((1, tn, CONV_W), tok)
    vec = pl.BlockSpec((1, 1, D_MODEL), per_b)
    return pl.pallas_call(
        _conv_proj_kernel,
        grid=(bn, n // tn),
        in_specs=[pl.BlockSpec((1, tn, D_MODEL), tok),
                  pl.BlockSpec((1, D_MODEL), lambda b, i: (0, 0)),
                  vec, vec,
                  pl.BlockSpec((D_MODEL, CONV_IN), lambda b, i: (0, 0))],
        out_specs=[narrow, narrow, narrow],
        out_shape=[jax.ShapeDtypeStruct((bn, n, CONV_W), F32)] * 3,
        compiler_params=_params("parallel", "parallel"),
        name="conv_proj",
    )(x, g.reshape(1, -1), shift, scale, w)


def _depthwise(win, w_ref, cols, taps, rows):
    offs = [CONV_HALO + k - taps // 2 for k in range(taps)]
    acc = None
    for b in range(8):
        ks = [k for k in range(taps) if offs[k] % 8 == b]
        if not ks:
            continue
        shifted = pltpu.roll(win, win.shape[0] - b, 0) if b else win
        for k in ks:
            term = shifted[offs[k] - b:offs[k] - b + rows, :] * w_ref[k:k + 1, cols]
            acc = term if acc is None else acc + term
    return acc


def _conv_kernel(v_ref, gb_ref, u_ref, scw_ref, cfw_ref, cfb_ref, lng_ref, lnb_ref, yc_ref, yd_ref, vpad_ref, upad_ref):
    i = pl.program_id(1)
    n = v_ref.shape[1]
    tn = gb_ref.shape[1]

    @pl.when(i == 0)
    def _stage():
        zeros = jnp.zeros((CONV_HALO, CONV_W), F32)
        for pad_ref, src_ref in ((vpad_ref, v_ref), (upad_ref, u_ref)):
            pad_ref[0:CONV_HALO, :] = zeros
            pad_ref[CONV_HALO:CONV_HALO + n, :] = src_ref[0]
            pad_ref[CONV_HALO + n:, :] = zeros

    for j in range(tn // CONV_SUB):
        rows = slice(j * CONV_SUB, (j + 1) * CONV_SUB)
        start = pl.multiple_of(i * tn + j * CONV_SUB, CONV_SUB)
        yd_parts = []
        for s in range(CONV_W // LANES):
            cols = slice(s * LANES, (s + 1) * LANES)
            vwin = vpad_ref[pl.ds(start, CONV_SUB + 2 * CONV_HALO), cols]
            yc = gb_ref[0, rows, cols] * _depthwise(vwin, scw_ref, cols, SC_K, CONV_SUB)
            yc_ref[0, rows, cols] = yc.astype(BF16)
            uwin = upad_ref[pl.ds(start, CONV_SUB + 2 * CONV_HALO), cols]
            yd_parts.append(_depthwise(uwin, cfw_ref, cols, CF_K, CONV_SUB) + cfb_ref[:, cols])
        u = jnp.concatenate(yd_parts, axis=-1)
        mu = jnp.mean(u, axis=-1, keepdims=True)
        d = u - mu
        var = jnp.mean(d * d, axis=-1, keepdims=True)
        ln = (d * lax.rsqrt(var + EPS)) * lng_ref[...] + lnb_ref[...]
        yd_ref[0, rows, :] = (ln * _sigmoid(ln)).astype(BF16)


def _convs(v, gb, u, sc_w, cf_w, cf_b, ln_g, ln_b):
    bn, n, w = v.shape
    tn = CONV_TN
    full = pl.BlockSpec((1, n, w), lambda b, i: (b, 0, 0))
    tile = pl.BlockSpec((1, tn, w), lambda b, i: (b, i, 0))
    vec = pl.BlockSpec((1, w), lambda b, i: (0, 0))
    return pl.pallas_call(
        _conv_kernel,
        grid=(bn, n // tn),
        in_specs=[full, tile, full,
                  pl.BlockSpec((SC_K, w), lambda b, i: (0, 0)),
                  pl.BlockSpec((CF_K, w), lambda b, i: (0, 0)),
                  vec, vec, vec],
        out_specs=[tile, tile],
        out_shape=[jax.ShapeDtypeStruct((bn, n, w), BF16)] * 2,
        scratch_shapes=[pltpu.VMEM((n + 2 * CONV_HALO, w), F32)] * 2,
        compiler_params=_params("parallel", "arbitrary"),
        name="convs",
    )(v, gb, u, sc_w, cf_w, cf_b.reshape(1, -1), ln_g.reshape(1, -1), ln_b.reshape(1, -1))


def _split_mod(mod, bn):
    return [mod[:bn, j * D_MODEL:(j + 1) * D_MODEL].reshape(bn, 1, D_MODEL) for j in range(6)]


def _moe(x, gate2, aff_t, h2, w1, w3, w2, final_g, final):
    pos, gsel, post = _route(aff_t)
    xg, gs = _gather(pos, gsel, h2)
    y = _expert_ffn(xg, gs, w1, w3, w2)
    return _combine(post, y, x, gate2, final_g, final)


def kernel(x, c, ctx, c_ctx, l0_norm1, l0_w_mod, l0_b_mod, l0_w_in, l0_w_out, l0_rpb, l0_lam_q1, l0_lam_k1, l0_lam_q2, l0_lam_k2, l0_subln, l0_norm2, l0_w_router, l0_w1, l0_w3, l0_w2, l1_norm1, l1_w_mod, l1_b_mod, l1_w_in, l1_w_out, l1_sc_w, l1_cf_w, l1_cf_b, l1_ln_g, l1_ln_b, l1_norm2, l1_w_router, l1_w1, l1_w3, l1_w2, final_norm):
    bn = x.shape[0]
    pad_rows = -(bn + 1) % 8
    cc = jnp.concatenate([c, c_ctx[None, :], jnp.zeros((pad_rows, D_MODEL), F32)], axis=0)

    mod0 = _modulation(cc, l0_w_mod, l0_b_mod)
    shift1, scale1, gate1, shift2, scale2, gate2 = _split_mod(mod0, bn)
    ctx_shift = jnp.broadcast_to(mod0[bn, :D_MODEL].reshape(1, 1, D_MODEL), (bn, 1, D_MODEL))
    ctx_scale = jnp.broadcast_to(mod0[bn, D_MODEL:2 * D_MODEL].reshape(1, 1, D_MODEL), (bn, 1, D_MODEL))
    w_in0 = l0_w_in.astype(BF16)
    cos_t, sin_t = _rope_tables()
    z = _attn_proj(x, l0_norm1, shift1, scale1, w_in0, cos_t, sin_t, rope_groups=(1, 3), scaled_groups=(0, 1))
    zc = _attn_proj(ctx, l0_norm1, ctx_shift, ctx_scale, w_in0[:, 2 * HEAD_GROUP:], cos_t[:CTX_LEN], sin_t[:CTX_LEN],
                    rope_groups=(), scaled_groups=())
    y_a = _na_attention(z, zc, _na_bias_table(l0_rpb))
    y_b = _diff_attention(z, zc, l0_lam_q1, l0_lam_k1, l0_lam_q2, l0_lam_k2, l0_subln)
    x1, h2, aff_t = _out_proj(x, y_a, y_b, l0_w_out.astype(BF16), gate1, l0_norm2, shift2, scale2,
                              l0_w_router.T.astype(BF16))
    x2 = _moe(x1, gate2, aff_t, h2, l0_w1, l0_w3, l0_w2, final_norm, final=False)

    mod1 = _modulation(cc, l1_w_mod, l1_b_mod)
    shift1b, scale1b, gate1b, shift2b, scale2b, gate2b = _split_mod(mod1, bn)
    v, gb, u = _conv_proj(x2, l1_norm1, shift1b, scale1b, l1_w_in.astype(BF16))
    y_c, y_d = _convs(v, gb, u, l1_sc_w, l1_cf_w, l1_cf_b, l1_ln_g, l1_ln_b)
    x3, h2b, aff_tb = _out_proj(x2, y_c, y_d, l1_w_out.astype(BF16), gate1b, l1_norm2, shift2b, scale2b,
                                l1_w_router.T.astype(BF16))
    return _moe(x3, gate2b, aff_tb, h2b, l1_w1, l1_w3, l1_w2, final_norm, final=True)
```
